```python
import jax, jax.numpy as jnp
from jax import lax
import numpy as np

D_MODEL = 2048
BATCH = 8
SEQ = 2048
DEPTH = 4

N_MIXERS = 4
N_SUB = 3
D_FF = 5632
EPS = 1e-6
LN_EPS = 1e-5
CONV_WIDTH = 31
M_HEADS = 8
M_QK_DIM = D_MODEL // 16
M_V_DIM = D_MODEL // M_HEADS
M_QK = M_HEADS * M_QK_DIM
M_V = M_HEADS * M_V_DIM
M_CONV_WIDTH = 4
M_CHUNK = 128
SB_HEADS = 16
SB_HEAD_DIM = D_MODEL // SB_HEADS
SB_BLOCK = 128
POOL_WINDOWS = (2, 4, 8, 16)
POOL_GROUP = D_MODEL // 4

kernel_name = "hybrid_interleaved_macaron_trunk"


def rms_norm(x, g):
    xf = x.astype(jnp.float32)
    y = xf * lax.rsqrt(jnp.mean(xf * xf, axis=-1, keepdims=True) + EPS)
    return (y * g.astype(jnp.float32)).astype(x.dtype)


def layer_norm(x, g, b):
    xf = x.astype(jnp.float32)
    mu = jnp.mean(xf, axis=-1, keepdims=True)
    var = jnp.mean(jnp.square(xf - mu), axis=-1, keepdims=True)
    y = (xf - mu) * lax.rsqrt(var + LN_EPS)
    return (y * g.astype(jnp.float32) + b.astype(jnp.float32)).astype(x.dtype)


def modulate(h, shift, scale):
    return h * (1 + scale[:, None, :]) + shift[:, None, :]


def swiglu_ffn(h, w13, w2):
    a, b = jnp.split(h @ w13, 2, axis=-1)
    return (jax.nn.silu(a) * b) @ w2


def causal_dw_conv(x, w, b):
    k = w.shape[0]
    y = lax.conv_general_dilated(
        x, w[:, None, :].astype(x.dtype), window_strides=(1,), padding=[(k - 1, 0)],
        dimension_numbers=("NWC", "WIO", "NWC"), feature_group_count=x.shape[-1])
    return y + b


def conformer_conv(h, w_in, dw, dw_b, ln_g, ln_b, w_out):
    a, g = jnp.split(h @ w_in, 2, axis=-1)
    u = a * jax.nn.sigmoid(g)
    u = causal_dw_conv(u, dw, dw_b)
    u = jax.nn.silu(layer_norm(u, ln_g, ln_b))
    return u @ w_out


def mlstm(h, w_in, b_gate, conv_w, conv_b, norm_g, w_out):
    bsz, seq, _ = h.shape
    f32 = jnp.float32
    proj = h @ w_in
    qk, v, o, gates = jnp.split(proj, [2 * M_QK, 2 * M_QK + M_V, 2 * M_QK + 2 * M_V], axis=-1)
    qk = jax.nn.silu(causal_dw_conv(qk, conv_w, conv_b))
    q, k = jnp.split(qk, 2, axis=-1)
    gates = gates.astype(f32) + b_gate.astype(f32)
    i_pre, f_pre = jnp.split(gates, 2, axis=-1)
    log_f = jax.nn.log_sigmoid(f_pre)
    nc = seq // M_CHUNK

    def to_chunks(t, d):
        return t.astype(f32).reshape(bsz, nc, M_CHUNK, M_HEADS, d).transpose(1, 0, 3, 2, 4)

    def gate_chunks(g):
        return g.reshape(bsz, nc, M_CHUNK, M_HEADS).transpose(1, 0, 3, 2)

    qc = to_chunks(q, M_QK_DIM) * (M_QK_DIM ** -0.5)
    kc = to_chunks(k, M_QK_DIM)
    vc = to_chunks(v, M_V_DIM)
    lic = gate_chunks(i_pre)
    lfc = gate_chunks(log_f)
    causal = jnp.tril(jnp.ones((M_CHUNK, M_CHUNK), dtype=bool))

    def step(carry, inp):
        c_st, n_st, m_st = carry
        qb, kb, vb, li, lf = inp
        b = jnp.cumsum(lf, axis=-1)
        dmat = jnp.where(causal, b[..., :, None] - b[..., None, :] + li[..., None, :], -jnp.inf)
        inter = b + m_st[..., None]
        m_t = jnp.maximum(inter, jnp.max(dmat, axis=-1))
        w_intra = jnp.exp(dmat - m_t[..., None])
        w_inter = jnp.exp(inter - m_t)
        s = jnp.einsum("bhtd,bhsd->bhts", qb, kb) * w_intra
        num = (w_inter[..., None] * jnp.einsum("bhtd,bhdv->bhtv", qb, c_st)
               + jnp.einsum("bhts,bhsv->bhtv", s, vb))
        den = w_inter * jnp.einsum("bhtd,bhd->bht", qb, n_st) + jnp.sum(s, axis=-1)
        hb = num / jnp.maximum(jnp.abs(den), jnp.exp(-m_t))[..., None]
        b_last = b[..., -1]
        g = b_last[..., None] - b + li
        m_new = jnp.maximum(b_last + m_st, jnp.max(g, axis=-1))
        decay = jnp.exp(b_last + m_st - m_new)
        wg = jnp.exp(g - m_new[..., None])
        c_new = decay[..., None, None] * c_st + jnp.einsum("bhs,bhsd,bhsv->bhdv", wg, kb, vb)
        n_new = decay[..., None] * n_st + jnp.einsum("bhs,bhsd->bhd", wg, kb)
        return (c_new, n_new, m_new), hb

    init = (jnp.zeros((bsz, M_HEADS, M_QK_DIM, M_V_DIM), f32),
            jnp.zeros((bsz, M_HEADS, M_QK_DIM), f32),
            jnp.zeros((bsz, M_HEADS), f32))
    _, hs = lax.scan(step, init, (qc, kc, vc, lic, lfc))
    hs = hs.transpose(1, 0, 3, 2, 4).reshape(bsz, seq, M_HEADS, M_V_DIM)
    hs = hs * lax.rsqrt(jnp.mean(hs * hs, axis=-1, keepdims=True) + EPS)
    hs = hs.reshape(bsz, seq, M_V) * norm_g.astype(f32)
    return (hs.astype(h.dtype) * jax.nn.sigmoid(o)) @ w_out


def stick_breaking_attention(h, w_in, w_out):
    bsz, seq, _ = h.shape
    q, k, v = jnp.split(h @ w_in, 3, axis=-1)

    def heads(t):
        return t.reshape(bsz, seq, SB_HEADS, SB_HEAD_DIM).transpose(0, 2, 1, 3)

    q, k, v = heads(q), heads(k), heads(v)
    scale = SB_HEAD_DIM ** -0.5
    outs = []
    for blk in range(seq // SB_BLOCK):
        t0 = blk * SB_BLOCK
        t1 = t0 + SB_BLOCK
        qb, kb, vb = q[:, :, t0:t1], k[:, :, :t1], v[:, :, :t1]
        z = jnp.einsum("bhtd,bhsd->bhts", qb, kb).astype(jnp.float32) * scale
        t_idx = t0 + jnp.arange(SB_BLOCK)[:, None]
        s_idx = jnp.arange(t1)[None, :]
        mask = s_idx < t_idx
        log_beta = jax.nn.log_sigmoid(z)
        log_keep = jnp.where(mask, jax.nn.log_sigmoid(-z), 0.0)
        later = lax.cumsum(log_keep, axis=3, reverse=True) - log_keep
        a = jnp.where(mask, jnp.exp(log_beta + later), 0.0)
        outs.append(jnp.einsum("bhts,bhsd->bhtd", a.astype(vb.dtype), vb))
    o = jnp.concatenate(outs, axis=2).transpose(0, 2, 1, 3).reshape(bsz, seq, D_MODEL)
    return o @ w_out


def multiscale_pool(h, w_in, w_grp, ch_scale, w_out):
    u = h @ w_in
    bsz, seq, _ = u.shape
    pos = jnp.arange(1, seq + 1, dtype=jnp.float32)[:, None]
    pooled = []
    for ug, w in zip(jnp.split(u, len(POOL_WINDOWS), axis=-1), POOL_WINDOWS):
        uf = ug.astype(jnp.float32)
        cs = jnp.cumsum(uf, axis=1)
        cs_prev = jnp.pad(cs, ((0, 0), (w, 0), (0, 0)))[:, :seq]
        mean = (cs - cs_prev) / jnp.minimum(pos, w)
        pooled.append(mean - uf)
    p = jnp.stack(pooled, axis=2).astype(h.dtype)
    y = jnp.einsum("bsgc,gcd->bsgd", p, w_grp).reshape(bsz, seq, D_MODEL)
    return (y * ch_scale) @ w_out


def _n_layers_of(m):
    return len(range(m, DEPTH, N_MIXERS))


def setup_inputs(seed: int = 0) -> dict:
    key = jax.random.key(seed)
    ks = iter(jax.random.split(key, 40))

    def nrm(shape, fan_in, mult=1.0):
        return jax.random.normal(next(ks), shape, jnp.float32) * (mult * fan_in ** -0.5)

    def gain(shape):
        return 1.0 + 0.05 * jax.random.normal(next(ks), shape, jnp.float32)

    def small(shape):
        return 0.02 * jax.random.normal(next(ks), shape, jnp.float32)

    na, nb, nc, nd = (_n_layers_of(m) for m in range(N_MIXERS))
    d = D_MODEL
    b_i = 0.1 * jax.random.normal(next(ks), (nb, M_HEADS), jnp.float32)
    b_f = 3.0 + 3.0 * jax.random.uniform(next(ks), (nb, M_HEADS), jnp.float32)
    return {
        "x": jax.random.normal(next(ks), (BATCH, SEQ, d), jnp.float32),
        "c": jax.random.normal(next(ks), (BATCH, d), jnp.float32),
        "ada_w": nrm((DEPTH, d, N_SUB * 3 * d), d, 0.5),
        "ada_b": small((DEPTH, N_SUB * 3 * d)),
        "norm_g": gain((DEPTH, N_SUB, d)),
        "ffn_w13": nrm((DEPTH, 2, d, 2 * D_FF), d),
        "ffn_w2": nrm((DEPTH, 2, D_FF, d), D_FF),
        "conv_w_in": nrm((na, d, 2 * d), d),
        "conv_dw": nrm((na, CONV_WIDTH, d), CONV_WIDTH),
        "conv_dw_b": small((na, d)),
        "conv_ln_g": gain((na, d)),
        "conv_ln_b": small((na, d)),
        "conv_w_out": nrm((na, d, d), d),
        "m_w_in": nrm((nb, d, 2 * M_QK + 2 * M_V + 2 * M_HEADS), d),
        "m_b_gate": jnp.concatenate([b_i, b_f], axis=-1),
        "m_conv_w": nrm((nb, M_CONV_WIDTH, 2 * M_QK), M_CONV_WIDTH),
        "m_conv_b": small((nb, 2 * M_QK)),
        "m_norm_g": gain((nb, M_V)),
        "m_w_out": nrm((nb, M_V, d), M_V),
        "sb_w_in": nrm((nc, d, 3 * d), d),
        "sb_w_out": nrm((nc, d, d), d),
        "p_w_in": nrm((nd, d, d), d),
        "p_w_grp": nrm((nd, len(POOL_WINDOWS), POOL_GROUP, POOL_GROUP), POOL_GROUP),
        "p_scale": gain((nd, d)),
        "p_w_out": nrm((nd, d, d), d),
        "final_g": gain((d,)),
    }


def reference(x, c, ada_w, ada_b, norm_g, ffn_w13, ffn_w2,
              conv_w_in, conv_dw, conv_dw_b, conv_ln_g, conv_ln_b, conv_w_out,
              m_w_in, m_b_gate, m_conv_w, m_conv_b, m_norm_g, m_w_out,
              sb_w_in, sb_w_out,
              p_w_in, p_w_grp, p_scale, p_w_out,
              final_g):
    bsz = x.shape[0]
    c_act = jax.nn.silu(c)
    for i in range(DEPTH):
        mix, j = i % N_MIXERS, i // N_MIXERS
        mod = (c_act @ ada_w[i] + ada_b[i]).reshape(bsz, N_SUB, 3, D_MODEL)

        h = modulate(rms_norm(x, norm_g[i, 0]), mod[:, 0, 0], mod[:, 0, 1])
        x = x + 0.5 * mod[:, 0, 2][:, None, :] * swiglu_ffn(h, ffn_w13[i, 0], ffn_w2[i, 0])

        h = modulate(rms_norm(x, norm_g[i, 1]), mod[:, 1, 0], mod[:, 1, 1])
        if mix == 0:
            y = conformer_conv(h, conv_w_in[j], conv_dw[j], conv_dw_b[j], conv_ln_g[j],
                               conv_ln_b[j], conv_w_out[j])
        elif mix == 1:
            y = mlstm(h, m_w_in[j], m_b_gate[j], m_conv_w[j], m_conv_b[j], m_norm_g[j], m_w_out[j])
        elif mix == 2:
            y = stick_breaking_attention(h, sb_w_in[j], sb_w_out[j])
        else:
            y = multiscale_pool(h, p_w_in[j], p_w_grp[j], p_scale[j], p_w_out[j])
        x = x + mod[:, 1, 2][:, None, :] * y

        h = modulate(rms_norm(x, norm_g[i, 2]), mod[:, 2, 0], mod[:, 2, 1])
        x = x + 0.5 * mod[:, 2, 2][:, None, :] * swiglu_ffn(h, ffn_w13[i, 1], ffn_w2[i, 1])
    return rms_norm(x, final_g)
```

```python
import functools

import jax
import jax.numpy as jnp
from jax import lax
from jax.experimental import pallas as pl
from jax.experimental.pallas import tpu as pltpu

F32 = jnp.float32
BF16 = jnp.bfloat16

EPS = 1e-6
LN_EPS = 1e-5
N_SUB = 3
N_MOD = 3 * N_SUB
CONV_WIDTH = 31
CONV_HALO = 32
M_HEADS = 8
M_CONV_WIDTH = 4
M_CHUNK = 128
SB_HEADS = 16
POOL_WINDOWS = (2, 4, 8, 16)
POOL_HALO = 16

VMEM_LIMIT_BYTES = 56 * 1024 * 1024
ROW_CHUNK = 256
COL_CHUNK = 512


def _params(*semantics):
    return pltpu.CompilerParams(dimension_semantics=semantics,
                                vmem_limit_bytes=VMEM_LIMIT_BYTES)


def _dot(a, b):
    return jnp.dot(a, b, preferred_element_type=F32)


def _dot_nt(a, b):
    return lax.dot_general(a, b, (((1,), (1,)), ((), ())), preferred_element_type=F32)


def _split_dot(x, w, rhs=True):
    hi = x.astype(BF16)
    lo = (x - hi.astype(F32)).astype(BF16)
    if rhs:
        return _dot(hi, w) + _dot(lo, w)
    return _dot(w, hi) + _dot(w, lo)


def _norm_mod_rows(x_ref, g_ref, shift_ref, scale_ref, h_ref):
    tm = x_ref.shape[0]
    rc = min(ROW_CHUNK, tm)
    g = g_ref[...]
    mult = 1.0 + scale_ref[...]
    shift = shift_ref[...]

    def body(r, carry):
        rows = pl.ds(pl.multiple_of(r * rc, rc), rc)
        xf = x_ref[rows, :]
        ms = jnp.mean(xf * xf, axis=-1, keepdims=True)
        y = xf * lax.rsqrt(ms + EPS) * g
        h_ref[rows, :] = (y * mult + shift).astype(h_ref.dtype)
        return carry

    lax.fori_loop(0, tm // rc, body, 0)


def _mod_spec(d, base, j, tiles_per_seq):
    return pl.BlockSpec(
        (None, 1, d), lambda m, *_: (base + (m // tiles_per_seq) * N_MOD + j, 0, 0))


def _ada_kernel(c_ref, w_ref, b_ref, o_ref):
    c_act = jax.nn.silu(c_ref[...]).astype(BF16)
    o_ref[...] = _dot(c_act, w_ref[...].astype(BF16)) + b_ref[...]


def _ada_mod(c, ada_w, ada_b, tn=1024):
    depth, d, n = ada_w.shape
    bsz = c.shape[0]
    tn = min(tn, n)
    out = pl.pallas_call(
        _ada_kernel,
        grid=(depth, n // tn),
        in_specs=[
            pl.BlockSpec((bsz, d), lambda i, j: (0, 0)),
            pl.BlockSpec((None, d, tn), lambda i, j: (i, 0, j)),
            pl.BlockSpec((None, 1, tn), lambda i, j: (i, 0, j)),
        ],
        out_specs=pl.BlockSpec((None, bsz, tn), lambda i, j: (i, 0, j)),
        out_shape=jax.ShapeDtypeStruct((depth, bsz, n), F32),
        compiler_params=_params("parallel", "parallel"),
        name="ada_mod",
    )(c, ada_w, ada_b.reshape(depth, 1, n))
    return out.reshape(depth * bsz * N_MOD, 1, d)


def _ffn_kernel(x_ref, shift_ref, scale_ref, gate_ref, g_ref, wa_ref, wb_ref, w2_ref,
                o_ref, h_ref):
    f = pl.program_id(1)
    tm, d = x_ref.shape
    cc = min(COL_CHUNK, d)

    @pl.when(f == 0)
    def _():
        _norm_mod_rows(x_ref, g_ref, shift_ref, scale_ref, h_ref)
        o_ref[...] = jnp.zeros_like(o_ref)

    h = h_ref[...]
    a = _dot(h, wa_ref[...])
    b = _dot(h, wb_ref[...])
    act = (jax.nn.silu(a) * b).astype(BF16)
    for n in range(d // cc):
        cols = slice(n * cc, (n + 1) * cc)
        o_ref[:, cols] += _dot(act, w2_ref[:, cols])

    @pl.when(f == pl.num_programs(1) - 1)
    def _():
        half_gate = 0.5 * gate_ref[...]
        rc = min(ROW_CHUNK, tm)

        def body(r, carry):
            rows = pl.ds(pl.multiple_of(r * rc, rc), rc)
            o_ref[rows, :] = x_ref[rows, :] + half_gate * o_ref[rows, :]
            return carry

        lax.fori_loop(0, tm // rc, body, 0)


def _ffn(x, mod, mod_base, norm_g, w13, w2, layer, which, seq, tm=1024, tf=512):
    t, d = x.shape
    ff = w2.shape[2]
    tm = min(tm, seq)
    tf = min(tf, ff)
    tps = seq // tm
    nf = ff // tf
    single = pl.Buffered(1)
    return pl.pallas_call(
        _ffn_kernel,
        grid=(t // tm, nf),
        in_specs=[
            pl.BlockSpec((tm, d), lambda m, f: (m, 0), pipeline_mode=single),
            _mod_spec(d, mod_base, 0, tps),
            _mod_spec(d, mod_base, 1, tps),
            _mod_spec(d, mod_base, 2, tps),
            pl.BlockSpec((1, d), lambda m, f: (0, 0)),
            pl.BlockSpec((None, None, d, tf), lambda m, f: (layer, which, 0, f)),
            pl.BlockSpec((None, None, d, tf), lambda m, f: (layer, which, 0, nf + f)),
            pl.BlockSpec((None, None, tf, d), lambda m, f: (layer, which, f, 0)),
        ],
        out_specs=pl.BlockSpec((tm, d), lambda m, f: (m, 0), pipeline_mode=single),
        out_shape=jax.ShapeDtypeStruct((t, d), F32),
        scratch_shapes=[pltpu.VMEM((tm, d), BF16)],
        compiler_params=_params("parallel", "arbitrary"),
        name="ffn",
    )(x, mod, mod, mod, norm_g, w13, w13, w2)


def _proj_kernel(x_ref, shift_ref, scale_ref, g_ref, w_ref, o_ref, h_ref):
    @pl.when(pl.program_id(1) == 0)
    def _():
        _norm_mod_rows(x_ref, g_ref, shift_ref, scale_ref, h_ref)

    o_ref[...] = _dot(h_ref[...], w_ref[...]).astype(o_ref.dtype)


def _proj_glu_kernel(x_ref, shift_ref, scale_ref, g_ref, wa_ref, wg_ref, o_ref, h_ref):
    @pl.when(pl.program_id(1) == 0)
    def _():
        _norm_mod_rows(x_ref, g_ref, shift_ref, scale_ref, h_ref)

    h = h_ref[...]
    a = _dot(h, wa_ref[...])
    g = _dot(h, wg_ref[...])
    o_ref[...] = (a * jax.nn.sigmoid(g)).astype(o_ref.dtype)


def _proj_gates_kernel(x_ref, shift_ref, scale_ref, g_ref, w_ref, wg_ref, wgt_ref,
                       o_ref, gates_ref, gates_t_ref, h_ref):
    @pl.when(pl.program_id(1) == 0)
    def _():
        _norm_mod_rows(x_ref, g_ref, shift_ref, scale_ref, h_ref)
        h = h_ref[...]
        gates_ref[...] = _dot(h, wg_ref[...])
        gates_t_ref[...] = _dot_nt(wgt_ref[...], h)

    o_ref[...] = _dot(h_ref[...], w_ref[...]).astype(o_ref.dtype)


def _proj(x, mod, mod_base, norm_g, w, seq, out_dtype, tm=1024, tn=512, glu=False,
          gate_w=None):
    t, d = x.shape
    n = w.shape[1] // 2 if glu else w.shape[1]
    tm = min(tm, seq)
    tn = min(tn, n)
    tps = seq // tm
    nn = n // tn
    x_spec = pl.BlockSpec((tm, d), lambda m, j: (m, 0))
    common = [x_spec, _mod_spec(d, mod_base, 0, tps), _mod_spec(d, mod_base, 1, tps),
              pl.BlockSpec((1, d), lambda m, j: (0, 0))]
    out_spec = pl.BlockSpec((tm, tn), lambda m, j: (m, j))
    out_shape = jax.ShapeDtypeStruct((t, n), out_dtype)
    scratch = [pltpu.VMEM((tm, d), BF16)]
    cp = _params("parallel", "arbitrary")
    if glu:
        return pl.pallas_call(
            _proj_glu_kernel, grid=(t // tm, nn),
            in_specs=common + [pl.BlockSpec((d, tn), lambda m, j: (0, j)),
                               pl.BlockSpec((d, tn), lambda m, j: (0, nn + j))],
            out_specs=out_spec, out_shape=out_shape, scratch_shapes=scratch,
            compiler_params=cp, name="proj_glu",
        )(x, mod, mod, norm_g, w, w)
    if gate_w is not None:
        ng = gate_w.shape[1]
        return pl.pallas_call(
            _proj_gates_kernel, grid=(t // tm, nn),
            in_specs=common + [pl.BlockSpec((d, tn), lambda m, j: (0, j)),
                               pl.BlockSpec((d, ng), lambda m, j: (0, 0)),
                               pl.BlockSpec((ng, d), lambda m, j: (0, 0))],
            out_specs=[out_spec,
                       pl.BlockSpec((tm, ng), lambda m, j: (m, 0)),
                       pl.BlockSpec((ng, tm), lambda m, j: (0, m))],
            out_shape=[out_shape,
                       jax.ShapeDtypeStruct((t, ng), F32),
                       jax.ShapeDtypeStruct((ng, t), F32)],
            scratch_shapes=scratch, compiler_params=cp, name="proj_gates",
        )(x, mod, mod, norm_g, w, gate_w, gate_w.T)
    return pl.pallas_call(
        _proj_kernel, grid=(t // tm, nn),
        in_specs=common + [pl.BlockSpec((d, tn), lambda m, j: (0, j))],
        out_specs=out_spec, out_shape=out_shape, scratch_shapes=scratch,
        compiler_params=cp, name="proj",
    )(x, mod, mod, norm_g, w)


def _gated_residual_cols(x_ref, u, gate_ref, w_ref, o_ref):
    d = x_ref.shape[1]
    cc = min(COL_CHUNK, d)
    for n in range(d // cc):
        cols = slice(n * cc, (n + 1) * cc)
        o_ref[:, cols] = x_ref[:, cols] + gate_ref[:, cols] * _dot(u, w_ref[:, cols])


def _outproj_kernel(x_ref, u_ref, gate_ref, w_ref, o_ref):
    _gated_residual_cols(x_ref, u_ref[...], gate_ref, w_ref, o_ref)


def _outproj(x, u, mod, mod_base, w, seq, tm=512):
    t, d = x.shape
    k = u.shape[1]
    tm = min(tm, seq)
    tps = seq // tm
    return pl.pallas_call(
        _outproj_kernel,
        grid=(t // tm,),
        in_specs=[
            pl.BlockSpec((tm, d), lambda m: (m, 0)),
            pl.BlockSpec((tm, k), lambda m: (m, 0)),
            _mod_spec(d, mod_base, 2, tps),
            pl.BlockSpec((k, d), lambda m: (0, 0), pipeline_mode=pl.Buffered(1)),
        ],
        out_specs=pl.BlockSpec((tm, d), lambda m: (m, 0)),
        out_shape=jax.ShapeDtypeStruct((t, d), F32),
        compiler_params=_params("parallel"),
        name="outproj",
    )(x, u, mod, w)


def _conv_core_kernel(u_ref, halo_ref, dw_ref, dwb_ref, lng_ref, lnb_ref, o_ref,
                      ext_ref, y_ref, *, tiles_per_seq):
    tm, d = u_ref.shape
    first = (pl.program_id(0) % tiles_per_seq) == 0
    ext_ref[0:CONV_HALO, :] = jnp.where(first, 0.0, halo_ref[...])
    ext_ref[CONV_HALO:, :] = u_ref[...]

    rc = 64
    lead = CONV_HALO - (CONV_WIDTH - 1)
    wrows = rc + CONV_HALO
    for cb in range(d // 128):
        cols = slice(cb * 128, (cb + 1) * 128)
        bias = dwb_ref[:, cols]

        def body(r, carry, cols=cols, bias=bias):
            base = pl.multiple_of(r * rc, rc)
            win = ext_ref[pl.ds(base, wrows), cols]
            acc = jnp.broadcast_to(bias, (rc, 128))
            for res in range(8):
                shifted = pltpu.roll(win, wrows - res, axis=0) if res else win
                for k in range(CONV_WIDTH):
                    if (lead + k) % 8 == res:
                        start = lead + k - res
                        acc = acc + shifted[start:start + rc, :] * dw_ref[k:k + 1, cols]
            y_ref[pl.ds(base, rc), cols] = acc
            return carry

        lax.fori_loop(0, tm // rc, body, 0)

    lrc = min(ROW_CHUNK, tm)
    ln_g = lng_ref[...]
    ln_b = lnb_ref[...]

    def ln_body(r, carry):
        rows = pl.ds(pl.multiple_of(r * lrc, lrc), lrc)
        y = y_ref[rows, :]
        mu = jnp.mean(y, axis=-1, keepdims=True)
        yc = y - mu
        var = jnp.mean(yc * yc, axis=-1, keepdims=True)
        z = yc * lax.rsqrt(var + LN_EPS) * ln_g + ln_b
        o_ref[rows, :] = jax.nn.silu(z).astype(o_ref.dtype)
        return carry

    lax.fori_loop(0, tm // lrc, ln_body, 0)


def _conv_core(u, dw, dw_b, ln_g, ln_b, seq, tm=512):
    t, d = u.shape
    tm = min(tm, seq)
    tps = seq // tm
    hb = tm // CONV_HALO
    return pl.pallas_call(
        functools.partial(_conv_core_kernel, tiles_per_seq=tps),
        grid=(t // tm,),
        in_specs=[
            pl.BlockSpec((tm, d), lambda m: (m, 0)),
            pl.BlockSpec((CONV_HALO, d), lambda m: (jnp.maximum(m * hb - 1, 0), 0)),
            pl.BlockSpec((CONV_WIDTH, d), lambda m: (0, 0)),
            pl.BlockSpec((1, d), lambda m: (0, 0)),
            pl.BlockSpec((1, d), lambda m: (0, 0)),
            pl.BlockSpec((1, d), lambda m: (0, 0)),
        ],
        out_specs=pl.BlockSpec((tm, d), lambda m: (m, 0)),
        out_shape=jax.ShapeDtypeStruct((t, d), BF16),
        scratch_shapes=[pltpu.VMEM((tm + CONV_HALO, d), F32), pltpu.VMEM((tm, d), F32)],
        compiler_params=_params("parallel"),
        name="conv_core",
    )(u, u, dw, dw_b, ln_g, ln_b)


def _mlstm_kernel(qk_ref, v_ref, o_ref, gates_ref, gates_t_ref, bg_ref, bgt_ref,
                  cw_ref, cb_ref, ng_ref, out_ref,
                  c_state, n_state, m_state, win_ref, *, dk, dv):
    lc = M_CHUNK
    nh = M_HEADS
    hist = 8

    @pl.when(pl.program_id(1) == 0)
    def _():
        c_state[...] = jnp.zeros_like(c_state)
        n_state[...] = jnp.zeros_like(n_state)
        m_state[...] = jnp.zeros_like(m_state)
        win_ref[0:hist, :] = jnp.zeros((hist, win_ref.shape[1]), F32)

    win_ref[hist:, :] = qk_ref[...]

    row = lax.broadcasted_iota(jnp.int32, (lc, lc), 0)
    col = lax.broadcasted_iota(jnp.int32, (lc, lc), 1)
    causal = col <= row
    tri = causal.astype(BF16)
    tri_t = (row <= col).astype(BF16)

    g_col = gates_ref[...] + bg_ref[...]
    g_row = gates_t_ref[...] + bgt_ref[...]
    li_col = g_col[:, :nh]
    li_row = g_row[:nh, :]
    lf_col = jax.nn.log_sigmoid(g_col[:, nh:])
    lf_row = jax.nn.log_sigmoid(g_row[nh:, :])
    b_col = _split_dot(lf_col, tri, rhs=False)
    b_row = _split_dot(lf_row, tri_t)

    qscale = dk ** -0.5

    def conv_silu(cols):
        acc = jnp.broadcast_to(cb_ref[:, cols], (lc, dk))
        for k in range(M_CONV_WIDTH):
            start = hist - (M_CONV_WIDTH - 1) + k
            acc = acc + win_ref[start:start + lc, cols] * cw_ref[k:k + 1, cols]
        return jax.nn.silu(acc)

    for h in range(nh):
        q = conv_silu(slice(h * dk, (h + 1) * dk)) * qscale
        k = conv_silu(slice((nh + h) * dk, (nh + h + 1) * dk))
        vcols = slice(h * dv, (h + 1) * dv)
        v = v_ref[:, vcols]
        q16 = q.astype(BF16)
        k16 = k.astype(BF16)
        v16 = v.astype(BF16)

        bc = b_col[:, h:h + 1]
        br = b_row[h:h + 1, :]
        m_prev = m_state[h:h + 1, 0:1]
        dmat = jnp.where(causal, bc - br + li_row[h:h + 1, :], -jnp.inf)
        inter = bc + m_prev
        m_t = jnp.maximum(inter, jnp.max(dmat, axis=-1, keepdims=True))
        w_intra = jnp.exp(dmat - m_t)
        w_inter = jnp.exp(inter - m_t)
        s = _dot_nt(q16, k16) * w_intra
        num = w_inter * _dot(q16, c_state[h].astype(BF16)) + _dot(s.astype(BF16), v16)
        qn = jnp.sum(q * n_state[h:h + 1, :], axis=-1, keepdims=True)
        den = w_inter * qn + jnp.sum(s, axis=-1, keepdims=True)
        hb = num / jnp.maximum(jnp.abs(den), jnp.exp(-m_t))
        hn = hb * lax.rsqrt(jnp.mean(hb * hb, axis=-1, keepdims=True) + EPS)
        hn = hn * ng_ref[:, vcols]
        out_ref[:, vcols] = (hn * jax.nn.sigmoid(o_ref[:, vcols])).astype(out_ref.dtype)

        b_last = br[:, lc - 1:lc]
        g = b_last - bc + li_col[:, h:h + 1]
        m_new = jnp.maximum(b_last + m_prev, jnp.max(g, axis=0, keepdims=True))
        decay = jnp.exp(b_last + m_prev - m_new)
        kw = k * jnp.exp(g - m_new)
        c_state[h] = decay * c_state[h] + _dot(kw.T.astype(BF16), v16)
        n_state[h:h + 1, :] = decay * n_state[h:h + 1, :] + jnp.sum(kw, axis=0, keepdims=True)
        m_state[h:h + 1, :] = jnp.broadcast_to(m_new, (1, m_state.shape[1]))

    win_ref[0:hist, :] = qk_ref[lc - hist:lc, :]


def _mlstm_core(proj, gates, gates_t, b_gate, conv_w, conv_b, norm_g, bsz, seq, dk, dv):
    t = proj.shape[0]
    nh = M_HEADS
    nc = seq // M_CHUNK
    qkw = 2 * nh * dk
    vw = nh * dv
    assert qkw == vw, "projection column blocks assume equal q|k and v widths"
    ng = 2 * nh
    const = lambda r, c: pl.BlockSpec((r, c), lambda b, i: (0, 0))
    return pl.pallas_call(
        functools.partial(_mlstm_kernel, dk=dk, dv=dv),
        grid=(bsz, nc),
        in_specs=[
            pl.BlockSpec((M_CHUNK, qkw), lambda b, i: (b * nc + i, 0)),
            pl.BlockSpec((M_CHUNK, vw), lambda b, i: (b * nc + i, 1)),
            pl.BlockSpec((M_CHUNK, vw), lambda b, i: (b * nc + i, 2)),
            pl.BlockSpec((M_CHUNK, ng), lambda b, i: (b * nc + i, 0)),
            pl.BlockSpec((ng, M_CHUNK), lambda b, i: (0, b * nc + i)),
            const(1, ng), const(ng, 1),
            const(M_CONV_WIDTH, qkw), const(1, qkw), const(1, vw),
        ],
        out_specs=pl.BlockSpec((M_CHUNK, vw), lambda b, i: (b * nc + i, 0)),
        out_shape=jax.ShapeDtypeStruct((t, vw), BF16),
        scratch_shapes=[
            pltpu.VMEM((nh, dk, dv), F32),
            pltpu.VMEM((nh, dk), F32),
            pltpu.VMEM((nh, 128), F32),
            pltpu.VMEM((M_CHUNK + 8, qkw), F32),
        ],
        compiler_params=_params("parallel", "arbitrary"),
        name="mlstm_core",
    )(proj, proj, proj, gates, gates_t, b_gate.reshape(1, ng), b_gate.reshape(ng, 1),
      conv_w, conv_b, norm_g)


def _sb_kernel(q_ref, k_ref, v_ref, o_ref, *, scale):
    tq, hd = q_ref.shape
    qi = pl.program_id(2)
    q = q_ref[...]
    row = lax.broadcasted_iota(jnp.int32, (tq, tq), 0)
    col = lax.broadcasted_iota(jnp.int32, (tq, tq), 1)
    strict = col < row
    after = (row > col).astype(BF16)

    def tile(kj, carry, acc, diag):
        rows = pl.ds(pl.multiple_of(kj * tq, tq), tq)
        z = _dot_nt(q, k_ref[rows, :]) * scale
        log_beta = jnp.minimum(z, 0.0) - jnp.log(1.0 + jnp.exp(-jnp.abs(z)))
        log_keep = log_beta - z
        if diag:
            log_keep = jnp.where(strict, log_keep, 0.0)
        later = _split_dot(log_keep, after) + carry
        a = jnp.exp(log_beta + later)
        if diag:
            a = jnp.where(strict, a, 0.0)
        acc = acc + _dot(a.astype(BF16), v_ref[rows, :])
        carry = carry + jnp.sum(log_keep, axis=-1, keepdims=True)
        return carry, acc

    carry, acc = tile(qi, jnp.zeros((tq, 1), F32), jnp.zeros((tq, hd), F32), True)

    def body(i, state):
        return tile(qi - 1 - i, state[0], state[1], False)

    carry, acc = lax.fori_loop(0, qi, body, (carry, acc))
    o_ref[...] = acc.astype(o_ref.dtype)


def _sb_core(qkv, bsz, seq, hd, tq=256):
    t = qkv.shape[0]
    nh = SB_HEADS
    tq = min(tq, seq)
    nq = seq // tq
    return pl.pallas_call(
        functools.partial(_sb_kernel, scale=hd ** -0.5),
        grid=(bsz, nh, nq),
        in_specs=[
            pl.BlockSpec((tq, hd), lambda b, h, i: (b * nq + i, h)),
            pl.BlockSpec((seq, hd), lambda b, h, i: (b, nh + h)),
            pl.BlockSpec((seq, hd), lambda b, h, i: (b, 2 * nh + h)),
        ],
        out_specs=pl.BlockSpec((tq, hd), lambda b, h, i: (b * nq + i, h)),
        out_shape=jax.ShapeDtypeStruct((t, nh * hd), BF16),
        compiler_params=_params("parallel", "parallel", "arbitrary"),
        name="sb_core",
    )(qkv, qkv, qkv)


def _pool_kernel(x_ref, u_ref, halo_ref, gate_ref, wg_ref, sc_ref, wo_ref, o_ref,
                 ext_ref, y_ref, *, tiles_per_seq):
    tm, d = u_ref.shape
    ngrp = len(POOL_WINDOWS)
    gw = d // ngrp
    tile_in_seq = pl.program_id(0) % tiles_per_seq
    ext_ref[0:POOL_HALO, :] = jnp.where(tile_in_seq == 0, 0.0, halo_ref[...])
    ext_ref[POOL_HALO:, :] = u_ref[...]
    pos = (tile_in_seq * tm + 1 + lax.broadcasted_iota(jnp.int32, (tm, 1), 0)).astype(F32)

    for gi, w in enumerate(POOL_WINDOWS):
        inv = 1.0 / jnp.minimum(pos, float(w))
        for cb in range(gw // 128):
            lo = gi * gw + cb * 128
            cols = slice(lo, lo + 128)
            ssum = ext_ref[:, cols]
            shift = 1
            while shift < w:
                ssum = ssum + pltpu.roll(ssum, shift, axis=0)
                shift *= 2
            uf = u_ref[:, cols]
            p = ssum[POOL_HALO:, :] * inv - uf
            y_ref[:, cols] = p.astype(y_ref.dtype)

    for gi in range(ngrp):
        cols = slice(gi * gw, (gi + 1) * gw)
        yg = _dot(y_ref[:, cols], wg_ref[gi]) * sc_ref[:, cols]
        y_ref[:, cols] = yg.astype(y_ref.dtype)

    _gated_residual_cols(x_ref, y_ref[...], gate_ref, wo_ref, o_ref)


def _pool_mix(x, u, mod, mod_base, w_grp, ch_scale, w_out, seq, tm=256):
    t, d = x.shape
    tm = min(tm, seq)
    tps = seq // tm
    hb = tm // POOL_HALO
    ngrp, gw, _ = w_grp.shape
    single = pl.Buffered(1)
    return pl.pallas_call(
        functools.partial(_pool_kernel, tiles_per_seq=tps),
        grid=(t // tm,),
        in_specs=[
            pl.BlockSpec((tm, d), lambda m: (m, 0)),
            pl.BlockSpec((tm, d), lambda m: (m, 0)),
            pl.BlockSpec((POOL_HALO, d), lambda m: (jnp.maximum(m * hb - 1, 0), 0)),
            _mod_spec(d, mod_base, 2, tps),
            pl.BlockSpec((ngrp, gw, gw), lambda m: (0, 0, 0), pipeline_mode=single),
            pl.BlockSpec((1, d), lambda m: (0, 0)),
            pl.BlockSpec((d, d), lambda m: (0, 0), pipeline_mode=single),
        ],
        out_specs=pl.BlockSpec((tm, d), lambda m: (m, 0)),
        out_shape=jax.ShapeDtypeStruct((t, d), F32),
        scratch_shapes=[pltpu.VMEM((tm + POOL_HALO, d), F32), pltpu.VMEM((tm, d), BF16)],
        compiler_params=_params("parallel"),
        name="pool_mix",
    )(x, u, u, mod, w_grp, ch_scale, w_out)


def _final_norm_kernel(x_ref, g_ref, o_ref):
    xf = x_ref[...]
    ms = jnp.mean(xf * xf, axis=-1, keepdims=True)
    o_ref[...] = xf * lax.rsqrt(ms + EPS) * g_ref[...]


def _final_norm(x, g, tm=256):
    t, d = x.shape
    tm = min(tm, t)
    return pl.pallas_call(
        _final_norm_kernel,
        grid=(t // tm,),
        in_specs=[pl.BlockSpec((tm, d), lambda m: (m, 0)),
                  pl.BlockSpec((1, d), lambda m: (0, 0))],
        out_specs=pl.BlockSpec((tm, d), lambda m: (m, 0)),
        out_shape=jax.ShapeDtypeStruct((t, d), F32),
        compiler_params=_params("parallel"),
        name="final_norm",
    )(x, g)


def kernel(x, c, ada_w, ada_b, norm_g, ffn_w13, ffn_w2, conv_w_in, conv_dw, conv_dw_b, conv_ln_g, conv_ln_b, conv_w_out, m_w_in, m_b_gate, m_conv_w, m_conv_b, m_norm_g, m_w_out, sb_w_in, sb_w_out, p_w_in, p_w_grp, p_scale, p_w_out, final_g):
    bsz, seq, d = x.shape
    depth = ada_w.shape[0]
    n_mixers = 4
    t = bsz * seq
    row = lambda a: a.reshape(1, -1)

    mod = _ada_mod(c, ada_w, ada_b)
    w13 = ffn_w13.astype(BF16)
    w2 = ffn_w2.astype(BF16)
    xt = x.reshape(t, d)

    for i in range(depth):
        mix, j = i % n_mixers, i // n_mixers
        base = i * bsz * N_MOD

        xt = _ffn(xt, mod, base, row(norm_g[i, 0]), w13, w2, i, 0, seq)

        mbase = base + 3
        g1 = row(norm_g[i, 1])
        if mix == 0:
            u = _proj(xt, mod, mbase, g1, conv_w_in[j].astype(BF16), seq, F32, glu=True)
            u = _conv_core(u, conv_dw[j], row(conv_dw_b[j]), row(conv_ln_g[j]),
                           row(conv_ln_b[j]), seq)
            xt = _outproj(xt, u, mod, mbase, conv_w_out[j].astype(BF16), seq)
        elif mix == 1:
            dv = d // M_HEADS
            dk = (m_w_in.shape[2] - 2 * d - 2 * M_HEADS) // (2 * M_HEADS)
            wide = 2 * M_HEADS * dk + 2 * d
            w_in = m_w_in[j].astype(BF16)
            proj, gates, gates_t = _proj(xt, mod, mbase, g1, w_in[:, :wide], seq, F32,
                                         gate_w=w_in[:, wide:])
            u = _mlstm_core(proj, gates, gates_t, m_b_gate[j], m_conv_w[j],
                            row(m_conv_b[j]), row(m_norm_g[j]), bsz, seq, dk, dv)
            xt = _outproj(xt, u, mod, mbase, m_w_out[j].astype(BF16), seq)
        elif mix == 2:
            qkv = _proj(xt, mod, mbase, g1, sb_w_in[j].astype(BF16), seq, BF16)
            u = _sb_core(qkv, bsz, seq, d // SB_HEADS)
            xt = _outproj(xt, u, mod, mbase, sb_w_out[j].astype(BF16), seq)
        else:
            u = _proj(xt, mod, mbase, g1, p_w_in[j].astype(BF16), seq, F32)
            xt = _pool_mix(xt, u, mod, mbase, p_w_grp[j].astype(BF16), row(p_scale[j]),
                           p_w_out[j].astype(BF16), seq)

        xt = _ffn(xt, mod, base + 6, row(norm_g[i, 2]), w13, w2, i, 1, seq)

    return _final_norm(xt, row(final_g)).reshape(bsz, seq, d)
```

```python
import functools

import jax
import jax.numpy as jnp
from jax import lax
from jax.experimental import pallas as pl
from jax.experimental.pallas import tpu as pltpu

F32 = jnp.float32
BF16 = jnp.bfloat16

EPS = 1e-6
LN_EPS = 1e-5
LOG2E = 1.4426950408889634
N_SUB = 3
N_MOD = 3 * N_SUB
CONV_WIDTH = 31
CONV_HALO = 32
M_HEADS = 8
M_CONV_WIDTH = 4
M_CHUNK = 128
SB_HEADS = 16
POOL_WINDOWS = (2, 4, 8, 16)
POOL_HALO = 16

VMEM_LIMIT_BYTES = 60 * 1024 * 1024
NORM_ROWS = 32
ROW_CHUNK = 256
COL_CHUNK = 512


def _params(*semantics):
    return pltpu.CompilerParams(dimension_semantics=semantics,
                                vmem_limit_bytes=VMEM_LIMIT_BYTES)


def _dot(a, b):
    return jnp.dot(a, b, preferred_element_type=F32)


def _dot_nt(a, b):
    return lax.dot_general(a, b, (((1,), (1,)), ((), ())), preferred_element_type=F32)


def _split_dot(x, w, rhs=True):
    hi = x.astype(BF16)
    lo = (x - hi.astype(F32)).astype(BF16)
    if rhs:
        return _dot(hi, w) + _dot(lo, w)
    return _dot(w, hi) + _dot(w, lo)


def _norm_mod_rows(x_ref, g_ref, shift_ref, scale_ref, h_ref):
    tm = x_ref.shape[0]
    rc = NORM_ROWS
    gain = g_ref[...] * (1.0 + scale_ref[...])
    shift = shift_ref[...]

    def body(r, carry):
        rows = pl.ds(pl.multiple_of(r * rc, rc), rc)
        xf = x_ref[rows, :]
        inv = lax.rsqrt(jnp.mean(xf * xf, axis=-1, keepdims=True) + EPS)
        h_ref[rows, :] = (x_ref[rows, :] * inv * gain + shift).astype(h_ref.dtype)
        return carry

    lax.fori_loop(0, tm // rc, body, 0, unroll=2)


def _mod_spec(d, base, j, tiles_per_seq):
    return pl.BlockSpec(
        (None, 1, d), lambda m, *_: (base + (m // tiles_per_seq) * N_MOD + j, 0, 0))


def _ada_kernel(c_ref, w_ref, b_ref, o_ref):
    c_act = jax.nn.silu(c_ref[...]).astype(BF16)
    o_ref[...] = _dot(c_act, w_ref[...].astype(BF16)) + b_ref[...]


def _ada_mod(c, ada_w, ada_b, tn=1024):
    depth, d, n = ada_w.shape
    bsz = c.shape[0]
    tn = min(tn, n)
    out = pl.pallas_call(
        _ada_kernel,
        grid=(depth, n // tn),
        in_specs=[
            pl.BlockSpec((bsz, d), lambda i, j: (0, 0)),
            pl.BlockSpec((None, d, tn), lambda i, j: (i, 0, j)),
            pl.BlockSpec((None, 1, tn), lambda i, j: (i, 0, j)),
        ],
        out_specs=pl.BlockSpec((None, bsz, tn), lambda i, j: (i, 0, j)),
        out_shape=jax.ShapeDtypeStruct((depth, bsz, n), F32),
        compiler_params=_params("parallel", "parallel"),
        name="ada_mod",
    )(c, ada_w, ada_b.reshape(depth, 1, n))
    return out.reshape(depth * bsz * N_MOD, 1, d)


def _ffn_kernel(x_ref, shift_ref, scale_ref, gate_ref, g_ref, wa_ref, wb_ref, w2_ref,
                o_ref, h_ref):
    f = pl.program_id(1)
    tm, d = x_ref.shape
    cc = min(COL_CHUNK, d)

    @pl.when(f == 0)
    def _():
        _norm_mod_rows(x_ref, g_ref, shift_ref, scale_ref, h_ref)
        o_ref[...] = jnp.zeros_like(o_ref)

    h = h_ref[...]
    a = _dot(h, wa_ref[...])
    b = _dot(h, wb_ref[...])
    act = (jax.nn.silu(a) * b).astype(BF16)
    for n in range(d // cc):
        cols = slice(n * cc, (n + 1) * cc)
        o_ref[:, cols] += _dot(act, w2_ref[:, cols])

    @pl.when(f == pl.num_programs(1) - 1)
    def _():
        half_gate = 0.5 * gate_ref[...]
        rc = min(ROW_CHUNK, tm)

        def body(r, carry):
            rows = pl.ds(pl.multiple_of(r * rc, rc), rc)
            o_ref[rows, :] = x_ref[rows, :] + half_gate * o_ref[rows, :]
            return carry

        lax.fori_loop(0, tm // rc, body, 0)


def _ffn(x, mod, mod_base, norm_g, w13, w2, layer, which, seq, tm=1024, tf=512):
    t, d = x.shape
    ff = w2.shape[2]
    tm = min(tm, seq)
    tf = min(tf, ff)
    tps = seq // tm
    nf = ff // tf
    return pl.pallas_call(
        _ffn_kernel,
        grid=(t // tm, nf),
        in_specs=[
            pl.BlockSpec((tm, d), lambda m, f: (m, 0)),
            _mod_spec(d, mod_base, 0, tps),
            _mod_spec(d, mod_base, 1, tps),
            _mod_spec(d, mod_base, 2, tps),
            pl.BlockSpec((1, d), lambda m, f: (0, 0)),
            pl.BlockSpec((None, None, d, tf), lambda m, f: (layer, which, 0, f)),
            pl.BlockSpec((None, None, d, tf), lambda m, f: (layer, which, 0, nf + f)),
            pl.BlockSpec((None, None, tf, d), lambda m, f: (layer, which, f, 0)),
        ],
        out_specs=pl.BlockSpec((tm, d), lambda m, f: (m, 0)),
        out_shape=jax.ShapeDtypeStruct((t, d), F32),
        scratch_shapes=[pltpu.VMEM((tm, d), BF16)],
        compiler_params=_params("parallel", "arbitrary"),
        name="ffn",
    )(x, mod, mod, mod, norm_g, w13, w13, w2)


def _proj_kernel(x_ref, shift_ref, scale_ref, g_ref, w_ref, o_ref, h_ref):
    @pl.when(pl.program_id(1) == 0)
    def _():
        _norm_mod_rows(x_ref, g_ref, shift_ref, scale_ref, h_ref)

    o_ref[...] = _dot(h_ref[...], w_ref[...]).astype(o_ref.dtype)


def _proj_glu_kernel(x_ref, shift_ref, scale_ref, g_ref, wa_ref, wg_ref, o_ref, h_ref):
    @pl.when(pl.program_id(1) == 0)
    def _():
        _norm_mod_rows(x_ref, g_ref, shift_ref, scale_ref, h_ref)

    h = h_ref[...]
    a = _dot(h, wa_ref[...])
    g = _dot(h, wg_ref[...])
    o_ref[...] = (a * jax.nn.sigmoid(g)).astype(o_ref.dtype)


def _proj_gates_kernel(x_ref, shift_ref, scale_ref, g_ref, w_ref, wg_ref, wgt_ref,
                       o_ref, gates_ref, gates_t_ref, h_ref):
    @pl.when(pl.program_id(1) == 0)
    def _():
        _norm_mod_rows(x_ref, g_ref, shift_ref, scale_ref, h_ref)
        h = h_ref[...]
        gates_ref[...] = _dot(h, wg_ref[...])
        gates_t_ref[...] = _dot_nt(wgt_ref[...], h)

    o_ref[...] = _dot(h_ref[...], w_ref[...]).astype(o_ref.dtype)


def _proj(x, mod, mod_base, norm_g, w, seq, out_dtype, tm=1024, tn=1024, glu=False,
          gate_w=None):
    t, d = x.shape
    n = w.shape[1] // 2 if glu else w.shape[1]
    tm = min(tm, seq)
    tn = min(tn // 2 if glu else tn, n)
    tps = seq // tm
    nn = n // tn
    x_spec = pl.BlockSpec((tm, d), lambda m, j: (m, 0))
    common = [x_spec, _mod_spec(d, mod_base, 0, tps), _mod_spec(d, mod_base, 1, tps),
              pl.BlockSpec((1, d), lambda m, j: (0, 0))]
    out_spec = pl.BlockSpec((tm, tn), lambda m, j: (m, j))
    out_shape = jax.ShapeDtypeStruct((t, n), out_dtype)
    scratch = [pltpu.VMEM((tm, d), BF16)]
    cp = _params("parallel", "arbitrary")
    if glu:
        return pl.pallas_call(
            _proj_glu_kernel, grid=(t // tm, nn),
            in_specs=common + [pl.BlockSpec((d, tn), lambda m, j: (0, j)),
                               pl.BlockSpec((d, tn), lambda m, j: (0, nn + j))],
            out_specs=out_spec, out_shape=out_shape, scratch_shapes=scratch,
            compiler_params=cp, name="proj_glu",
        )(x, mod, mod, norm_g, w, w)
    if gate_w is not None:
        ng = gate_w.shape[1]
        return pl.pallas_call(
            _proj_gates_kernel, grid=(t // tm, nn),
            in_specs=common + [pl.BlockSpec((d, tn), lambda m, j: (0, j)),
                               pl.BlockSpec((d, ng), lambda m, j: (0, 0)),
                               pl.BlockSpec((ng, d), lambda m, j: (0, 0))],
            out_specs=[out_spec,
                       pl.BlockSpec((tm, ng), lambda m, j: (m, 0)),
                       pl.BlockSpec((ng, tm), lambda m, j: (0, m))],
            out_shape=[out_shape,
                       jax.ShapeDtypeStruct((t, ng), F32),
                       jax.ShapeDtypeStruct((ng, t), F32)],
            scratch_shapes=scratch, compiler_params=cp, name="proj_gates",
        )(x, mod, mod, norm_g, w, gate_w, gate_w.T)
    return pl.pallas_call(
        _proj_kernel, grid=(t // tm, nn),
        in_specs=common + [pl.BlockSpec((d, tn), lambda m, j: (0, j))],
        out_specs=out_spec, out_shape=out_shape, scratch_shapes=scratch,
        compiler_params=cp, name="proj",
    )(x, mod, mod, norm_g, w)


def _gated_residual_cols(x_ref, u, gate_ref, w_ref, o_ref):
    d = x_ref.shape[1]
    cc = min(COL_CHUNK, d)
    for n in range(d // cc):
        cols = slice(n * cc, (n + 1) * cc)
        o_ref[:, cols] = x_ref[:, cols] + gate_ref[:, cols] * _dot(u, w_ref[:, cols])


def _outproj_kernel(x_ref, u_ref, gate_ref, w_ref, o_ref):
    _gated_residual_cols(x_ref, u_ref[...], gate_ref, w_ref, o_ref)


def _outproj(x, u, mod, mod_base, w, seq, tm=512):
    t, d = x.shape
    k = u.shape[1]
    tm = min(tm, seq)
    tps = seq // tm
    return pl.pallas_call(
        _outproj_kernel,
        grid=(t // tm,),
        in_specs=[
            pl.BlockSpec((tm, d), lambda m: (m, 0)),
            pl.BlockSpec((tm, k), lambda m: (m, 0)),
            _mod_spec(d, mod_base, 2, tps),
            pl.BlockSpec((k, d), lambda m: (0, 0), pipeline_mode=pl.Buffered(1)),
        ],
        out_specs=pl.BlockSpec((tm, d), lambda m: (m, 0)),
        out_shape=jax.ShapeDtypeStruct((t, d), F32),
        compiler_params=_params("parallel"),
        name="outproj",
    )(x, u, mod, w)


def _conv_core_kernel(u_ref, halo_ref, dw_ref, dwb_ref, lng_ref, lnb_ref, o_ref,
                      ext_ref, y_ref, *, tiles_per_seq):
    tm, d = u_ref.shape
    first = (pl.program_id(0) % tiles_per_seq) == 0
    ext_ref[0:CONV_HALO, :] = jnp.where(first, 0.0, halo_ref[...])
    ext_ref[CONV_HALO:, :] = u_ref[...]

    rc = 64
    lead = CONV_HALO - (CONV_WIDTH - 1)
    wrows = rc + CONV_HALO
    for cb in range(d // 128):
        cols = slice(cb * 128, (cb + 1) * 128)
        bias = dwb_ref[:, cols]

        def body(r, carry, cols=cols, bias=bias):
            base = pl.multiple_of(r * rc, rc)
            win = ext_ref[pl.ds(base, wrows), cols]
            acc = jnp.broadcast_to(bias, (rc, 128))
            for res in range(8):
                shifted = pltpu.roll(win, wrows - res, axis=0) if res else win
                for k in range(CONV_WIDTH):
                    if (lead + k) % 8 == res:
                        start = lead + k - res
                        acc = acc + shifted[start:start + rc, :] * dw_ref[k:k + 1, cols]
            y_ref[pl.ds(base, rc), cols] = acc
            return carry

        lax.fori_loop(0, tm // rc, body, 0)

    lrc = min(ROW_CHUNK, tm)
    ln_g = lng_ref[...]
    ln_b = lnb_ref[...]

    def ln_body(r, carry):
        rows = pl.ds(pl.multiple_of(r * lrc, lrc), lrc)
        y = y_ref[rows, :]
        mu = jnp.mean(y, axis=-1, keepdims=True)
        yc = y - mu
        var = jnp.mean(yc * yc, axis=-1, keepdims=True)
        z = yc * lax.rsqrt(var + LN_EPS) * ln_g + ln_b
        o_ref[rows, :] = jax.nn.silu(z).astype(o_ref.dtype)
        return carry

    lax.fori_loop(0, tm // lrc, ln_body, 0)


def _conv_core(u, dw, dw_b, ln_g, ln_b, seq, tm=512):
    t, d = u.shape
    tm = min(tm, seq)
    tps = seq // tm
    hb = tm // CONV_HALO
    return pl.pallas_call(
        functools.partial(_conv_core_kernel, tiles_per_seq=tps),
        grid=(t // tm,),
        in_specs=[
            pl.BlockSpec((tm, d), lambda m: (m, 0)),
            pl.BlockSpec((CONV_HALO, d), lambda m: (jnp.maximum(m * hb - 1, 0), 0)),
            pl.BlockSpec((CONV_WIDTH, d), lambda m: (0, 0)),
            pl.BlockSpec((1, d), lambda m: (0, 0)),
            pl.BlockSpec((1, d), lambda m: (0, 0)),
            pl.BlockSpec((1, d), lambda m: (0, 0)),
        ],
        out_specs=pl.BlockSpec((tm, d), lambda m: (m, 0)),
        out_shape=jax.ShapeDtypeStruct((t, d), BF16),
        scratch_shapes=[pltpu.VMEM((tm + CONV_HALO, d), F32), pltpu.VMEM((tm, d), F32)],
        compiler_params=_params("parallel"),
        name="conv_core",
    )(u, u, dw, dw_b, ln_g, ln_b)


def _mlstm_kernel(qk_ref, v_ref, o_ref, gates_ref, gates_t_ref, bg_ref, bgt_ref,
                  cw_ref, cb_ref, ng_ref, out_ref,
                  c_state, n_state, m_state, win_ref, *, dk, dv):
    lc = M_CHUNK
    nh = M_HEADS
    hist = 8

    @pl.when(pl.program_id(1) == 0)
    def _():
        c_state[...] = jnp.zeros_like(c_state)
        n_state[...] = jnp.zeros_like(n_state)
        m_state[...] = jnp.zeros_like(m_state)
        win_ref[0:hist, :] = jnp.zeros((hist, win_ref.shape[1]), F32)

    win_ref[hist:, :] = qk_ref[...]

    row = lax.broadcasted_iota(jnp.int32, (lc, lc), 0)
    col = lax.broadcasted_iota(jnp.int32, (lc, lc), 1)
    causal = col <= row
    tri = causal.astype(BF16)
    tri_t = (row <= col).astype(BF16)

    g_col = gates_ref[...] + bg_ref[...]
    g_row = gates_t_ref[...] + bgt_ref[...]
    li_col = g_col[:, :nh]
    li_row = g_row[:nh, :]
    lf_col = jax.nn.log_sigmoid(g_col[:, nh:])
    lf_row = jax.nn.log_sigmoid(g_row[nh:, :])
    b_col = _split_dot(lf_col, tri, rhs=False)
    b_row = _split_dot(lf_row, tri_t)

    qscale = dk ** -0.5

    def conv_silu(cols):
        acc = jnp.broadcast_to(cb_ref[:, cols], (lc, dk))
        for k in range(M_CONV_WIDTH):
            start = hist - (M_CONV_WIDTH - 1) + k
            acc = acc + win_ref[start:start + lc, cols] * cw_ref[k:k + 1, cols]
        return jax.nn.silu(acc)

    first = []
    for h in range(nh):
        q = conv_silu(slice(h * dk, (h + 1) * dk)) * qscale
        k = conv_silu(slice((nh + h) * dk, (nh + h + 1) * dk))
        q16 = q.astype(BF16)
        v16 = v_ref[:, h * dv:(h + 1) * dv].astype(BF16)
        bc = b_col[:, h:h + 1]
        br = b_row[h:h + 1, :]
        m_prev = m_state[h:h + 1, 0:1]
        b_last = br[:, lc - 1:lc]
        g = b_last - bc + li_col[:, h:h + 1]
        m_new = jnp.maximum(b_last + m_prev, jnp.max(g, axis=0, keepdims=True))
        decay = jnp.exp(b_last + m_prev - m_new)
        kw = k * jnp.exp(g - m_new)
        first.append(dict(
            v16=v16, bc=bc, br=br, m_prev=m_prev, m_new=m_new, decay=decay,
            qk=_dot_nt(q16, k.astype(BF16)),
            qc=_dot(q16, c_state[h].astype(BF16)),
            kv=_dot(kw.T.astype(BF16), v16),
            qn=jnp.sum(q * n_state[h:h + 1, :], axis=-1, keepdims=True),
            kw_sum=jnp.sum(kw, axis=0, keepdims=True)))

    second = []
    for h in range(nh):
        f = first[h]
        dmat = jnp.where(causal, f["bc"] - f["br"] + li_row[h:h + 1, :], -jnp.inf)
        inter = f["bc"] + f["m_prev"]
        m_t = jnp.maximum(inter, jnp.max(dmat, axis=-1, keepdims=True))
        w_inter = jnp.exp(inter - m_t)
        s = f["qk"] * jnp.exp(dmat - m_t)
        second.append(dict(
            m_t=m_t, w_inter=w_inter, s_sum=jnp.sum(s, axis=-1, keepdims=True),
            sv=_dot(s.astype(BF16), f["v16"])))

    for h in range(nh):
        f, s2 = first[h], second[h]
        vcols = slice(h * dv, (h + 1) * dv)
        num = s2["w_inter"] * f["qc"] + s2["sv"]
        den = s2["w_inter"] * f["qn"] + s2["s_sum"]
        hb = num / jnp.maximum(jnp.abs(den), jnp.exp(-s2["m_t"]))
        hn = hb * lax.rsqrt(jnp.mean(hb * hb, axis=-1, keepdims=True) + EPS)
        hn = hn * ng_ref[:, vcols]
        out_ref[:, vcols] = (hn * jax.nn.sigmoid(o_ref[:, vcols])).astype(out_ref.dtype)
        c_state[h] = f["decay"] * c_state[h] + f["kv"]
        n_state[h:h + 1, :] = f["decay"] * n_state[h:h + 1, :] + f["kw_sum"]
        m_state[h:h + 1, :] = jnp.broadcast_to(f["m_new"], (1, m_state.shape[1]))

    win_ref[0:hist, :] = qk_ref[lc - hist:lc, :]


def _mlstm_core(proj, gates, gates_t, b_gate, conv_w, conv_b, norm_g, bsz, seq, dk, dv):
    t = proj.shape[0]
    nh = M_HEADS
    nc = seq // M_CHUNK
    qkw = 2 * nh * dk
    vw = nh * dv
    assert qkw == vw, "projection column blocks assume equal q|k and v widths"
    ng = 2 * nh
    const = lambda r, c: pl.BlockSpec((r, c), lambda b, i: (0, 0))
    return pl.pallas_call(
        functools.partial(_mlstm_kernel, dk=dk, dv=dv),
        grid=(bsz, nc),
        in_specs=[
            pl.BlockSpec((M_CHUNK, qkw), lambda b, i: (b * nc + i, 0)),
            pl.BlockSpec((M_CHUNK, vw), lambda b, i: (b * nc + i, 1)),
            pl.BlockSpec((M_CHUNK, vw), lambda b, i: (b * nc + i, 2)),
            pl.BlockSpec((M_CHUNK, ng), lambda b, i: (b * nc + i, 0)),
            pl.BlockSpec((ng, M_CHUNK), lambda b, i: (0, b * nc + i)),
            const(1, ng), const(ng, 1),
            const(M_CONV_WIDTH, qkw), const(1, qkw), const(1, vw),
        ],
        out_specs=pl.BlockSpec((M_CHUNK, vw), lambda b, i: (b * nc + i, 0)),
        out_shape=jax.ShapeDtypeStruct((t, vw), BF16),
        scratch_shapes=[
            pltpu.VMEM((nh, dk, dv), F32),
            pltpu.VMEM((nh, dk), F32),
            pltpu.VMEM((nh, 128), F32),
            pltpu.VMEM((M_CHUNK + 8, qkw), F32),
        ],
        compiler_params=_params("parallel", "arbitrary"),
        name="mlstm_core",
    )(proj, proj, proj, gates, gates_t, b_gate.reshape(1, ng), b_gate.reshape(ng, 1),
      conv_w, conv_b, norm_g)


def _sb_kernel(q_ref, k_ref, v_ref, o_ref, *, scale, hd):
    tq = q_ref.shape[0]
    ng = q_ref.shape[1] // hd
    qi = pl.program_id(2)
    row = lax.broadcasted_iota(jnp.int32, (tq, tq), 0)
    col = lax.broadcasted_iota(jnp.int32, (tq, tq), 1)
    strict = col < row
    after = (row > col).astype(BF16)
    after2 = jnp.concatenate([after, after], axis=0)

    def tile(kj, state, diag):
        rows = pl.ds(pl.multiple_of(kj * tq, tq), tq)
        heads = [slice(g * hd, (g + 1) * hd) for g in range(ng)]
        zs = [_dot_nt(q_ref[:, cols], k_ref[rows, cols]) * (scale * LOG2E) for cols in heads]
        mids = []
        for g in range(ng):
            z = zs[g]
            log_beta = jnp.minimum(z, 0.0) - jnp.log2(1.0 + jnp.exp2(-jnp.abs(z)))
            log_keep = log_beta - z
            if diag:
                log_keep = jnp.where(strict, log_keep, 0.0)
            hi = log_keep.astype(BF16)
            lo = (log_keep - hi.astype(F32)).astype(BF16)
            within = _dot(jnp.concatenate([hi, lo], axis=1), after2)
            mids.append((log_beta, within, jnp.sum(log_keep, axis=-1, keepdims=True)))
        out = []
        for g in range(ng):
            carry, acc = state[g]
            log_beta, within, row_sum = mids[g]
            a = jnp.exp2(log_beta + within + carry)
            if diag:
                a = jnp.where(strict, a, 0.0)
            acc = acc + _dot(a.astype(BF16), v_ref[rows, heads[g]])
            out.append((carry + row_sum, acc))
        return tuple(out)

    init = tuple((jnp.zeros((tq, 1), F32), jnp.zeros((tq, hd), F32)) for _ in range(ng))
    state = tile(qi, init, True)
    state = lax.fori_loop(0, qi, lambda i, st: tile(qi - 1 - i, st, False), state)
    for g in range(ng):
        o_ref[:, g * hd:(g + 1) * hd] = state[g][1].astype(o_ref.dtype)


def _sb_core(qkv, bsz, seq, hd, tq=256, heads_per_step=4):
    t = qkv.shape[0]
    nh = SB_HEADS
    tq = min(tq, seq)
    nq = seq // tq
    ngrp = nh // heads_per_step
    gw = heads_per_step * hd
    return pl.pallas_call(
        functools.partial(_sb_kernel, scale=hd ** -0.5, hd=hd),
        grid=(bsz, ngrp, nq),
        in_specs=[
            pl.BlockSpec((tq, gw), lambda b, h, i: (b * nq + i, h)),
            pl.BlockSpec((seq, gw), lambda b, h, i: (b, ngrp + h)),
            pl.BlockSpec((seq, gw), lambda b, h, i: (b, 2 * ngrp + h)),
        ],
        out_specs=pl.BlockSpec((tq, gw), lambda b, h, i: (b * nq + i, h)),
        out_shape=jax.ShapeDtypeStruct((t, nh * hd), BF16),
        compiler_params=_params("parallel", "parallel", "arbitrary"),
        name="sb_core",
    )(qkv, qkv, qkv)


def _pool_kernel(x_ref, u_ref, halo_ref, gate_ref, wg_ref, sc_ref, wo_ref, o_ref,
                 ext_ref, y_ref, *, tiles_per_seq):
    tm, d = u_ref.shape
    ngrp = len(POOL_WINDOWS)
    gw = d // ngrp
    tile_in_seq = pl.program_id(0) % tiles_per_seq
    ext_ref[0:POOL_HALO, :] = jnp.where(tile_in_seq == 0, 0.0, halo_ref[...])
    ext_ref[POOL_HALO:, :] = u_ref[...]
    pos = (tile_in_seq * tm + 1 + lax.broadcasted_iota(jnp.int32, (tm, 1), 0)).astype(F32)

    for gi, w in enumerate(POOL_WINDOWS):
        inv = 1.0 / jnp.minimum(pos, float(w))
        for cb in range(gw // 128):
            lo = gi * gw + cb * 128
            cols = slice(lo, lo + 128)
            ssum = ext_ref[:, cols]
            shift = 1
            while shift < w:
                ssum = ssum + pltpu.roll(ssum, shift, axis=0)
                shift *= 2
            uf = u_ref[:, cols]
            p = ssum[POOL_HALO:, :] * inv - uf
            y_ref[:, cols] = p.astype(y_ref.dtype)

    for gi in range(ngrp):
        cols = slice(gi * gw, (gi + 1) * gw)
        yg = _dot(y_ref[:, cols], wg_ref[gi]) * sc_ref[:, cols]
        y_ref[:, cols] = yg.astype(y_ref.dtype)

    _gated_residual_cols(x_ref, y_ref[...], gate_ref, wo_ref, o_ref)


def _pool_mix(x, u, mod, mod_base, w_grp, ch_scale, w_out, seq, tm=256):
    t, d = x.shape
    tm = min(tm, seq)
    tps = seq // tm
    hb = tm // POOL_HALO
    ngrp, gw, _ = w_grp.shape
    single = pl.Buffered(1)
    return pl.pallas_call(
        functools.partial(_pool_kernel, tiles_per_seq=tps),
        grid=(t // tm,),
        in_specs=[
            pl.BlockSpec((tm, d), lambda m: (m, 0)),
            pl.BlockSpec((tm, d), lambda m: (m, 0)),
            pl.BlockSpec((POOL_HALO, d), lambda m: (jnp.maximum(m * hb - 1, 0), 0)),
            _mod_spec(d, mod_base, 2, tps),
            pl.BlockSpec((ngrp, gw, gw), lambda m: (0, 0, 0), pipeline_mode=single),
            pl.BlockSpec((1, d), lambda m: (0, 0)),
            pl.BlockSpec((d, d), lambda m: (0, 0), pipeline_mode=single),
        ],
        out_specs=pl.BlockSpec((tm, d), lambda m: (m, 0)),
        out_shape=jax.ShapeDtypeStruct((t, d), F32),
        scratch_shapes=[pltpu.VMEM((tm + POOL_HALO, d), F32), pltpu.VMEM((tm, d), BF16)],
        compiler_params=_params("parallel"),
        name="pool_mix",
    )(x, u, u, mod, w_grp, ch_scale, w_out)


def _final_norm_kernel(x_ref, g_ref, o_ref):
    xf = x_ref[...]
    ms = jnp.mean(xf * xf, axis=-1, keepdims=True)
    o_ref[...] = xf * lax.rsqrt(ms + EPS) * g_ref[...]


def _final_norm(x, g, tm=256):
    t, d = x.shape
    tm = min(tm, t)
    return pl.pallas_call(
        _final_norm_kernel,
        grid=(t // tm,),
        in_specs=[pl.BlockSpec((tm, d), lambda m: (m, 0)),
                  pl.BlockSpec((1, d), lambda m: (0, 0))],
        out_specs=pl.BlockSpec((tm, d), lambda m: (m, 0)),
        out_shape=jax.ShapeDtypeStruct((t, d), F32),
        compiler_params=_params("parallel"),
        name="final_norm",
    )(x, g)


def kernel(x, c, ada_w, ada_b, norm_g, ffn_w13, ffn_w2, conv_w_in, conv_dw, conv_dw_b, conv_ln_g, conv_ln_b, conv_w_out, m_w_in, m_b_gate, m_conv_w, m_conv_b, m_norm_g, m_w_out, sb_w_in, sb_w_out, p_w_in, p_w_grp, p_scale, p_w_out, final_g):
    bsz, seq, d = x.shape
    depth = ada_w.shape[0]
    n_mixers = 4
    t = bsz * seq
    row = lambda a: a.reshape(1, -1)

    mod = _ada_mod(c, ada_w, ada_b)
    w13 = ffn_w13.astype(BF16)
    w2 = ffn_w2.astype(BF16)
    xt = x.reshape(t, d)

    for i in range(depth):
        mix, j = i % n_mixers, i // n_mixers
        base = i * bsz * N_MOD

        xt = _ffn(xt, mod, base, row(norm_g[i, 0]), w13, w2, i, 0, seq)

        mbase = base + 3
        g1 = row(norm_g[i, 1])
        if mix == 0:
            u = _proj(xt, mod, mbase, g1, conv_w_in[j].astype(BF16), seq, F32, glu=True)
            u = _conv_core(u, conv_dw[j], row(conv_dw_b[j]), row(conv_ln_g[j]),
                           row(conv_ln_b[j]), seq)
            xt = _outproj(xt, u, mod, mbase, conv_w_out[j].astype(BF16), seq)
        elif mix == 1:
            dv = d // M_HEADS
            dk = (m_w_in.shape[2] - 2 * d - 2 * M_HEADS) // (2 * M_HEADS)
            wide = 2 * M_HEADS * dk + 2 * d
            w_in = m_w_in[j].astype(BF16)
            proj, gates, gates_t = _proj(xt, mod, mbase, g1, w_in[:, :wide], seq, F32,
                                         gate_w=w_in[:, wide:])
            u = _mlstm_core(proj, gates, gates_t, m_b_gate[j], m_conv_w[j],
                            row(m_conv_b[j]), row(m_norm_g[j]), bsz, seq, dk, dv)
            xt = _outproj(xt, u, mod, mbase, m_w_out[j].astype(BF16), seq)
        elif mix == 2:
            qkv = _proj(xt, mod, mbase, g1, sb_w_in[j].astype(BF16), seq, BF16)
            u = _sb_core(qkv, bsz, seq, d // SB_HEADS)
            xt = _outproj(xt, u, mod, mbase, sb_w_out[j].astype(BF16), seq)
        else:
            u = _proj(xt, mod, mbase, g1, p_w_in[j].astype(BF16), seq, F32)
            xt = _pool_mix(xt, u, mod, mbase, p_w_grp[j].astype(BF16), row(p_scale[j]),
                           p_w_out[j].astype(BF16), seq)

        xt = _ffn(xt, mod, base + 6, row(norm_g[i, 2]), w13, w2, i, 1, seq)

    return _final_norm(xt, row(final_g)).reshape(bsz, seq, d)
```

```python
import functools

import jax
import jax.numpy as jnp
from jax import lax
from jax.experimental import pallas as pl
from jax.experimental.pallas import tpu as pltpu

F32 = jnp.float32
BF16 = jnp.bfloat16

EPS = 1e-6
LN_EPS = 1e-5
LOG2E = 1.4426950408889634
N_SUB = 3
N_MOD = 3 * N_SUB
CONV_WIDTH = 31
CONV_HALO = 32
M_HEADS = 8
M_CONV_WIDTH = 4
M_CHUNK = 128
SB_HEADS = 16
SB_DEAD_LOG2 = -160.0
POOL_WINDOWS = (2, 4, 8, 16)
POOL_HALO = 16

VMEM_LIMIT_BYTES = 60 * 1024 * 1024
NORM_ROWS = 32
FIRST_STEP_ROWS = 256
ROW_CHUNK = 256
COL_CHUNK = 512


def _params(*semantics):
    return pltpu.CompilerParams(dimension_semantics=semantics,
                                vmem_limit_bytes=VMEM_LIMIT_BYTES)


def _dot(a, b):
    return jnp.dot(a, b, preferred_element_type=F32)


def _dot_nt(a, b):
    return lax.dot_general(a, b, (((1,), (1,)), ((), ())), preferred_element_type=F32)


def _split_dot(x, w, rhs=True):
    hi = x.astype(BF16)
    lo = (x - hi.astype(F32)).astype(BF16)
    if rhs:
        return _dot(hi, w) + _dot(lo, w)
    return _dot(w, hi) + _dot(w, lo)


def _norm_mod_rows(x_ref, g_ref, shift_ref, scale_ref, h_ref, start=None, size=None):
    rc = NORM_ROWS
    gain = g_ref[...] * (1.0 + scale_ref[...])
    shift = shift_ref[...]

    def chunk(rows):
        xf = x_ref[rows, :]
        inv = lax.rsqrt(jnp.mean(xf * xf, axis=-1, keepdims=True) + EPS)
        h_ref[rows, :] = (x_ref[rows, :] * inv * gain + shift).astype(h_ref.dtype)

    if start is not None:
        for r in range(size // rc):
            chunk(slice(start + r * rc, start + (r + 1) * rc))
        return

    def body(r, carry):
        chunk(pl.ds(pl.multiple_of(r * rc, rc), rc))
        return carry

    lax.fori_loop(0, x_ref.shape[0] // rc, body, 0, unroll=2)


def _mod_spec(d, base, j, tiles_per_seq):
    return pl.BlockSpec(
        (None, 1, d), lambda m, *_: (base + (m // tiles_per_seq) * N_MOD + j, 0, 0))


def _ada_kernel(c_ref, w_ref, b_ref, o_ref):
    c_act = jax.nn.silu(c_ref[...]).astype(BF16)
    o_ref[...] = _dot(c_act, w_ref[...].astype(BF16)) + b_ref[...]


def _ada_mod(c, ada_w, ada_b, tn=1024):
    depth, d, n = ada_w.shape
    bsz = c.shape[0]
    tn = min(tn, n)
    out = pl.pallas_call(
        _ada_kernel,
        grid=(depth, n // tn),
        in_specs=[
            pl.BlockSpec((bsz, d), lambda i, j: (0, 0)),
            pl.BlockSpec((None, d, tn), lambda i, j: (i, 0, j)),
            pl.BlockSpec((None, 1, tn), lambda i, j: (i, 0, j)),
        ],
        out_specs=pl.BlockSpec((None, bsz, tn), lambda i, j: (i, 0, j)),
        out_shape=jax.ShapeDtypeStruct((depth, bsz, n), F32),
        compiler_params=_params("parallel", "parallel"),
        name="ada_mod",
    )(c, ada_w, ada_b.reshape(depth, 1, n))
    return out.reshape(depth * bsz * N_MOD, 1, d)


def _ffn_kernel(*refs, final_norm, cast_next):
    x_ref, shift_ref, scale_ref, gate_ref, g_ref, wa_ref, wb_ref, w2_ref = refs[:8]
    rest = list(refs[8:])
    fg_ref = rest.pop(0) if final_norm else None
    next_in = (rest.pop(0), rest.pop(0)) if cast_next else ()
    o_ref = rest.pop(0)
    next_out = (rest.pop(0), rest.pop(0)) if cast_next else ()
    (h_ref,) = rest

    f = pl.program_id(1)
    tm, d = x_ref.shape
    cc = min(COL_CHUNK, d)

    for src, dst in zip(next_in, next_out):
        dst[...] = src[...].astype(dst.dtype)

    def swiglu_rows(rows, first):
        h = h_ref[rows, :]
        a = _dot(h, wa_ref[...])
        b = _dot(h, wb_ref[...])
        act = (jax.nn.silu(a) * b).astype(BF16)
        for n in range(d // cc):
            cols = slice(n * cc, (n + 1) * cc)
            y = _dot(act, w2_ref[:, cols])
            if first:
                o_ref[rows, cols] = y
            else:
                o_ref[rows, cols] += y

    @pl.when(f == 0)
    def _():
        fc = min(FIRST_STEP_ROWS, tm)
        for c in range(tm // fc):
            _norm_mod_rows(x_ref, g_ref, shift_ref, scale_ref, h_ref, c * fc, fc)
            swiglu_rows(slice(c * fc, (c + 1) * fc), True)

    @pl.when(f > 0)
    def _():
        swiglu_rows(slice(None), False)

    @pl.when(f == pl.num_programs(1) - 1)
    def _():
        half_gate = 0.5 * gate_ref[...]
        rc = NORM_ROWS if final_norm else min(ROW_CHUNK, tm)

        def body(r, carry):
            rows = pl.ds(pl.multiple_of(r * rc, rc), rc)
            y = x_ref[rows, :] + half_gate * o_ref[rows, :]
            if final_norm:
                y = y * lax.rsqrt(jnp.mean(y * y, axis=-1, keepdims=True) + EPS) * fg_ref[...]
            o_ref[rows, :] = y
            return carry

        lax.fori_loop(0, tm // rc, body, 0)


def _ffn(x, mod, mod_base, norm_g, w13, w2, seq, next_w=None, final_g=None, tm=1024, tf=512):
    t, d = x.shape
    ff = w2.shape[0]
    tm = min(tm, seq)
    tf = min(tf, ff)
    tps = seq // tm
    nm = t // tm
    nf = ff // tf
    in_specs = [
        pl.BlockSpec((tm, d), lambda m, f: (m, 0)),
        _mod_spec(d, mod_base, 0, tps),
        _mod_spec(d, mod_base, 1, tps),
        _mod_spec(d, mod_base, 2, tps),
        pl.BlockSpec((1, d), lambda m, f: (0, 0)),
        pl.BlockSpec((d, tf), lambda m, f: (0, f)),
        pl.BlockSpec((d, tf), lambda m, f: (0, nf + f)),
        pl.BlockSpec((tf, d), lambda m, f: (f, 0)),
    ]
    args = [x, mod, mod, mod, norm_g, w13, w13, w2]
    out_specs = [pl.BlockSpec((tm, d), lambda m, f: (m, 0))]
    out_shape = [jax.ShapeDtypeStruct((t, d), F32)]
    if final_g is not None:
        in_specs.append(pl.BlockSpec((1, d), lambda m, f: (0, 0)))
        args.append(final_g)
    if next_w is not None:
        stack13, stack2, layer, which = next_w
        r13, c13 = d // nm, 2 * ff // nf
        r2, c2 = ff // nf, d // nm
        in_specs += [
            pl.BlockSpec((None, None, r13, c13), lambda m, f: (layer, which, m, f)),
            pl.BlockSpec((None, None, r2, c2), lambda m, f: (layer, which, f, m)),
        ]
        args += [stack13, stack2]
        out_specs += [pl.BlockSpec((r13, c13), lambda m, f: (m, f)),
                      pl.BlockSpec((r2, c2), lambda m, f: (f, m))]
        out_shape += [jax.ShapeDtypeStruct((d, 2 * ff), BF16),
                      jax.ShapeDtypeStruct((ff, d), BF16)]
    out = pl.pallas_call(
        functools.partial(_ffn_kernel, final_norm=final_g is not None,
                          cast_next=next_w is not None),
        grid=(nm, nf),
        in_specs=in_specs,
        out_specs=out_specs,
        out_shape=out_shape,
        scratch_shapes=[pltpu.VMEM((tm, d), BF16)],
        compiler_params=_params("parallel", "arbitrary"),
        name="ffn",
    )(*args)
    return out if next_w is not None else out[0]


def _proj_steps(x_ref, shift_ref, scale_ref, g_ref, h_ref, emit, emit_first=None):
    tm = x_ref.shape[0]

    @pl.when(pl.program_id(1) == 0)
    def _():
        fc = min(FIRST_STEP_ROWS, tm)
        for c in range(tm // fc):
            rows = slice(c * fc, (c + 1) * fc)
            _norm_mod_rows(x_ref, g_ref, shift_ref, scale_ref, h_ref, c * fc, fc)
            emit(rows)
            if emit_first is not None:
                emit_first(rows)

    @pl.when(pl.program_id(1) > 0)
    def _():
        emit(slice(None))


def _proj_kernel(x_ref, shift_ref, scale_ref, g_ref, w_ref, o_ref, h_ref):
    def emit(rows):
        o_ref[rows, :] = _dot(h_ref[rows, :], w_ref[...]).astype(o_ref.dtype)

    _proj_steps(x_ref, shift_ref, scale_ref, g_ref, h_ref, emit)


def _proj_glu_kernel(x_ref, shift_ref, scale_ref, g_ref, wa_ref, wg_ref, o_ref, h_ref):
    def emit(rows):
        h = h_ref[rows, :]
        a = _dot(h, wa_ref[...])
        g = _dot(h, wg_ref[...])
        o_ref[rows, :] = (a * jax.nn.sigmoid(g)).astype(o_ref.dtype)

    _proj_steps(x_ref, shift_ref, scale_ref, g_ref, h_ref, emit)


def _proj_gates_kernel(x_ref, shift_ref, scale_ref, g_ref, w_ref, wg_ref, wgt_ref,
                       o_ref, gates_ref, gates_t_ref, h_ref):
    def emit(rows):
        o_ref[rows, :] = _dot(h_ref[rows, :], w_ref[...]).astype(o_ref.dtype)

    def emit_gates(rows):
        h = h_ref[rows, :]
        gates_ref[rows, :] = _dot(h, wg_ref[...])
        gates_t_ref[:, rows] = _dot_nt(wgt_ref[...], h)

    _proj_steps(x_ref, shift_ref, scale_ref, g_ref, h_ref, emit, emit_gates)


def _proj(x, mod, mod_base, norm_g, w, seq, out_dtype, tm=1024, tn=1024, glu=False,
          gate_w=None):
    t, d = x.shape
    n = w.shape[1] // 2 if glu else w.shape[1]
    tm = min(tm, seq)
    tn = min(tn // 2 if glu else tn, n)
    tps = seq // tm
    nn = n // tn
    x_spec = pl.BlockSpec((tm, d), lambda m, j: (m, 0))
    common = [x_spec, _mod_spec(d, mod_base, 0, tps), _mod_spec(d, mod_base, 1, tps),
              pl.BlockSpec((1, d), lambda m, j: (0, 0))]
    out_spec = pl.BlockSpec((tm, tn), lambda m, j: (m, j))
    out_shape = jax.ShapeDtypeStruct((t, n), out_dtype)
    scratch = [pltpu.VMEM((tm, d), BF16)]
    cp = _params("parallel", "arbitrary")
    if glu:
        return pl.pallas_call(
            _proj_glu_kernel, grid=(t // tm, nn),
            in_specs=common + [pl.BlockSpec((d, tn), lambda m, j: (0, j)),
                               pl.BlockSpec((d, tn), lambda m, j: (0, nn + j))],
            out_specs=out_spec, out_shape=out_shape, scratch_shapes=scratch,
            compiler_params=cp, name="proj_glu",
        )(x, mod, mod, norm_g, w, w)
    if gate_w is not None:
        ng = gate_w.shape[1]
        return pl.pallas_call(
            _proj_gates_kernel, grid=(t // tm, nn),
            in_specs=common + [pl.BlockSpec((d, tn), lambda m, j: (0, j)),
                               pl.BlockSpec((d, ng), lambda m, j: (0, 0)),
                               pl.BlockSpec((ng, d), lambda m, j: (0, 0))],
            out_specs=[out_spec,
                       pl.BlockSpec((tm, ng), lambda m, j: (m, 0)),
                       pl.BlockSpec((ng, tm), lambda m, j: (0, m))],
            out_shape=[out_shape,
                       jax.ShapeDtypeStruct((t, ng), F32),
                       jax.ShapeDtypeStruct((ng, t), F32)],
            scratch_shapes=scratch, compiler_params=cp, name="proj_gates",
        )(x, mod, mod, norm_g, w, gate_w, gate_w.T)
    return pl.pallas_call(
        _proj_kernel, grid=(t // tm, nn),
        in_specs=common + [pl.BlockSpec((d, tn), lambda m, j: (0, j))],
        out_specs=out_spec, out_shape=out_shape, scratch_shapes=scratch,
        compiler_params=cp, name="proj",
    )(x, mod, mod, norm_g, w)


def _gated_residual_cols(x_ref, u, gate_ref, w_ref, o_ref):
    d = x_ref.shape[1]
    cc = min(COL_CHUNK, d)
    for n in range(d // cc):
        cols = slice(n * cc, (n + 1) * cc)
        o_ref[:, cols] = x_ref[:, cols] + gate_ref[:, cols] * _dot(u, w_ref[:, cols])


def _outproj_kernel(x_ref, u_ref, gate_ref, w_ref, o_ref):
    _gated_residual_cols(x_ref, u_ref[...], gate_ref, w_ref, o_ref)


def _outproj(x, u, mod, mod_base, w, seq, tm=512):
    t, d = x.shape
    k = u.shape[1]
    tm = min(tm, seq)
    tps = seq // tm
    return pl.pallas_call(
        _outproj_kernel,
        grid=(t // tm,),
        in_specs=[
            pl.BlockSpec((tm, d), lambda m: (m, 0)),
            pl.BlockSpec((tm, k), lambda m: (m, 0)),
            _mod_spec(d, mod_base, 2, tps),
            pl.BlockSpec((k, d), lambda m: (0, 0), pipeline_mode=pl.Buffered(1)),
        ],
        out_specs=pl.BlockSpec((tm, d), lambda m: (m, 0)),
        out_shape=jax.ShapeDtypeStruct((t, d), F32),
        compiler_params=_params("parallel"),
        name="outproj",
    )(x, u, mod, w)


def _conv_core_kernel(u_ref, halo_ref, dw_ref, dwb_ref, lng_ref, lnb_ref, o_ref,
                      ext_ref, y_ref, *, tiles_per_seq):
    tm, d = u_ref.shape
    first = (pl.program_id(0) % tiles_per_seq) == 0
    ext_ref[0:CONV_HALO, :] = jnp.where(first, 0.0, halo_ref[...])
    ext_ref[CONV_HALO:, :] = u_ref[...]

    rc = 64
    lead = CONV_HALO - (CONV_WIDTH - 1)
    wrows = rc + CONV_HALO
    for cb in range(d // 128):
        cols = slice(cb * 128, (cb + 1) * 128)
        bias = dwb_ref[:, cols]

        def body(r, carry, cols=cols, bias=bias):
            base = pl.multiple_of(r * rc, rc)
            win = ext_ref[pl.ds(base, wrows), cols]
            acc = jnp.broadcast_to(bias, (rc, 128))
            for res in range(8):
                shifted = pltpu.roll(win, wrows - res, axis=0) if res else win
                for k in range(CONV_WIDTH):
                    if (lead + k) % 8 == res:
                        start = lead + k - res
                        acc = acc + shifted[start:start + rc, :] * dw_ref[k:k + 1, cols]
            y_ref[pl.ds(base, rc), cols] = acc
            return carry

        lax.fori_loop(0, tm // rc, body, 0)

    lrc = min(ROW_CHUNK, tm)
    ln_g = lng_ref[...]
    ln_b = lnb_ref[...]

    def ln_body(r, carry):
        rows = pl.ds(pl.multiple_of(r * lrc, lrc), lrc)
        y = y_ref[rows, :]
        mu = jnp.mean(y, axis=-1, keepdims=True)
        yc = y - mu
        var = jnp.mean(yc * yc, axis=-1, keepdims=True)
        z = yc * lax.rsqrt(var + LN_EPS) * ln_g + ln_b
        o_ref[rows, :] = jax.nn.silu(z).astype(o_ref.dtype)
        return carry

    lax.fori_loop(0, tm // lrc, ln_body, 0)


def _conv_core(u, dw, dw_b, ln_g, ln_b, seq, tm=512):
    t, d = u.shape
    tm = min(tm, seq)
    tps = seq // tm
    hb = tm // CONV_HALO
    return pl.pallas_call(
        functools.partial(_conv_core_kernel, tiles_per_seq=tps),
        grid=(t // tm,),
        in_specs=[
            pl.BlockSpec((tm, d), lambda m: (m, 0)),
            pl.BlockSpec((CONV_HALO, d), lambda m: (jnp.maximum(m * hb - 1, 0), 0)),
            pl.BlockSpec((CONV_WIDTH, d), lambda m: (0, 0)),
            pl.BlockSpec((1, d), lambda m: (0, 0)),
            pl.BlockSpec((1, d), lambda m: (0, 0)),
            pl.BlockSpec((1, d), lambda m: (0, 0)),
        ],
        out_specs=pl.BlockSpec((tm, d), lambda m: (m, 0)),
        out_shape=jax.ShapeDtypeStruct((t, d), BF16),
        scratch_shapes=[pltpu.VMEM((tm + CONV_HALO, d), F32), pltpu.VMEM((tm, d), F32)],
        compiler_params=_params("parallel"),
        name="conv_core",
    )(u, u, dw, dw_b, ln_g, ln_b)


def _mlstm_kernel(qk_ref, v_ref, o_ref, gates_ref, gates_t_ref, bg_ref, bgt_ref,
                  cw_ref, cb_ref, ng_ref, out_ref,
                  c_state, n_state, m_state, win_ref, *, dk, dv):
    lc = M_CHUNK
    nh = M_HEADS
    hist = 8

    @pl.when(pl.program_id(1) == 0)
    def _():
        c_state[...] = jnp.zeros_like(c_state)
        n_state[...] = jnp.zeros_like(n_state)
        m_state[...] = jnp.zeros_like(m_state)
        win_ref[0:hist, :] = jnp.zeros((hist, win_ref.shape[1]), F32)

    win_ref[hist:, :] = qk_ref[...]

    row = lax.broadcasted_iota(jnp.int32, (lc, lc), 0)
    col = lax.broadcasted_iota(jnp.int32, (lc, lc), 1)
    causal = col <= row
    tri = causal.astype(BF16)
    tri_t = (row <= col).astype(BF16)

    g_col = gates_ref[...] + bg_ref[...]
    g_row = gates_t_ref[...] + bgt_ref[...]
    li_col = g_col[:, :nh]
    li_row = g_row[:nh, :]
    lf_col = jax.nn.log_sigmoid(g_col[:, nh:])
    lf_row = jax.nn.log_sigmoid(g_row[nh:, :])
    b_col = _split_dot(lf_col, tri, rhs=False)
    b_row = _split_dot(lf_row, tri_t)

    qscale = dk ** -0.5

    def conv_silu(cols):
        acc = jnp.broadcast_to(cb_ref[:, cols], (lc, dk))
        for k in range(M_CONV_WIDTH):
            start = hist - (M_CONV_WIDTH - 1) + k
            acc = acc + win_ref[start:start + lc, cols] * cw_ref[k:k + 1, cols]
        return jax.nn.silu(acc)

    first = []
    for h in range(nh):
        q = conv_silu(slice(h * dk, (h + 1) * dk)) * qscale
        k = conv_silu(slice((nh + h) * dk, (nh + h + 1) * dk))
        q16 = q.astype(BF16)
        v16 = v_ref[:, h * dv:(h + 1) * dv].astype(BF16)
        bc = b_col[:, h:h + 1]
        br = b_row[h:h + 1, :]
        m_prev = m_state[h:h + 1, 0:1]
        b_last = br[:, lc - 1:lc]
        g = b_last - bc + li_col[:, h:h + 1]
        m_new = jnp.maximum(b_last + m_prev, jnp.max(g, axis=0, keepdims=True))
        decay = jnp.exp(b_last + m_prev - m_new)
        kw = k * jnp.exp(g - m_new)
        first.append(dict(
            v16=v16, bc=bc, br=br, m_prev=m_prev, m_new=m_new, decay=decay,
            qk=_dot_nt(q16, k.astype(BF16)),
            qc=_dot(q16, c_state[h].astype(BF16)),
            kv=_dot(kw.T.astype(BF16), v16),
            qn=jnp.sum(q * n_state[h:h + 1, :], axis=-1, keepdims=True),
            kw_sum=jnp.sum(kw, axis=0, keepdims=True)))

    second = []
    for h in range(nh):
        f = first[h]
        dmat = jnp.where(causal, f["bc"] - f["br"] + li_row[h:h + 1, :], -jnp.inf)
        inter = f["bc"] + f["m_prev"]
        m_t = jnp.maximum(inter, jnp.max(dmat, axis=-1, keepdims=True))
        w_inter = jnp.exp(inter - m_t)
        s = f["qk"] * jnp.exp(dmat - m_t)
        second.append(dict(
            m_t=m_t, w_inter=w_inter, s_sum=jnp.sum(s, axis=-1, keepdims=True),
            sv=_dot(s.astype(BF16), f["v16"])))

    for h in range(nh):
        f, s2 = first[h], second[h]
        vcols = slice(h * dv, (h + 1) * dv)
        num = s2["w_inter"] * f["qc"] + s2["sv"]
        den = s2["w_inter"] * f["qn"] + s2["s_sum"]
        hb = num / jnp.maximum(jnp.abs(den), jnp.exp(-s2["m_t"]))
        hn = hb * lax.rsqrt(jnp.mean(hb * hb, axis=-1, keepdims=True) + EPS)
        hn = hn * ng_ref[:, vcols]
        out_ref[:, vcols] = (hn * jax.nn.sigmoid(o_ref[:, vcols])).astype(out_ref.dtype)
        c_state[h] = f["decay"] * c_state[h] + f["kv"]
        n_state[h:h + 1, :] = f["decay"] * n_state[h:h + 1, :] + f["kw_sum"]
        m_state[h:h + 1, :] = jnp.broadcast_to(f["m_new"], (1, m_state.shape[1]))

    win_ref[0:hist, :] = qk_ref[lc - hist:lc, :]


def _mlstm_core(proj, gates, gates_t, b_gate, conv_w, conv_b, norm_g, bsz, seq, dk, dv):
    t = proj.shape[0]
    nh = M_HEADS
    nc = seq // M_CHUNK
    qkw = 2 * nh * dk
    vw = nh * dv
    assert qkw == vw, "projection column blocks assume equal q|k and v widths"
    ng = 2 * nh
    const = lambda r, c: pl.BlockSpec((r, c), lambda b, i: (0, 0))
    return pl.pallas_call(
        functools.partial(_mlstm_kernel, dk=dk, dv=dv),
        grid=(bsz, nc),
        in_specs=[
            pl.BlockSpec((M_CHUNK, qkw), lambda b, i: (b * nc + i, 0)),
            pl.BlockSpec((M_CHUNK, vw), lambda b, i: (b * nc + i, 1)),
            pl.BlockSpec((M_CHUNK, vw), lambda b, i: (b * nc + i, 2)),
            pl.BlockSpec((M_CHUNK, ng), lambda b, i: (b * nc + i, 0)),
            pl.BlockSpec((ng, M_CHUNK), lambda b, i: (0, b * nc + i)),
            const(1, ng), const(ng, 1),
            const(M_CONV_WIDTH, qkw), const(1, qkw), const(1, vw),
        ],
        out_specs=pl.BlockSpec((M_CHUNK, vw), lambda b, i: (b * nc + i, 0)),
        out_shape=jax.ShapeDtypeStruct((t, vw), BF16),
        scratch_shapes=[
            pltpu.VMEM((nh, dk, dv), F32),
            pltpu.VMEM((nh, dk), F32),
            pltpu.VMEM((nh, 128), F32),
            pltpu.VMEM((M_CHUNK + 8, qkw), F32),
        ],
        compiler_params=_params("parallel", "arbitrary"),
        name="mlstm_core",
    )(proj, proj, proj, gates, gates_t, b_gate.reshape(1, ng), b_gate.reshape(ng, 1),
      conv_w, conv_b, norm_g)


def _sb_kernel(q_ref, k_ref, v_ref, o_ref, *, scale, hd):
    tq = q_ref.shape[0]
    ng = q_ref.shape[1] // hd
    qi = pl.program_id(2)
    row = lax.broadcasted_iota(jnp.int32, (tq, tq), 0)
    col = lax.broadcasted_iota(jnp.int32, (tq, tq), 1)
    strict = col < row
    after = (row > col).astype(BF16)
    after2 = jnp.concatenate([after, after], axis=0)

    def tile(kj, state, diag):
        rows = pl.ds(pl.multiple_of(kj * tq, tq), tq)
        heads = [slice(g * hd, (g + 1) * hd) for g in range(ng)]
        zs = [_dot_nt(q_ref[:, cols], k_ref[rows, cols]) * (scale * LOG2E) for cols in heads]
        mids = []
        for g in range(ng):
            z = zs[g]
            log_beta = jnp.minimum(z, 0.0) - jnp.log2(1.0 + jnp.exp2(-jnp.abs(z)))
            log_keep = log_beta - z
            if diag:
                log_keep = jnp.where(strict, log_keep, 0.0)
            hi = log_keep.astype(BF16)
            lo = (log_keep - hi.astype(F32)).astype(BF16)
            within = _dot(jnp.concatenate([hi, lo], axis=1), after2)
            mids.append((log_beta, within, within[:, 0:1] + log_keep[:, 0:1]))
        out = []
        for g in range(ng):
            carry, acc = state[g]
            log_beta, within, row_sum = mids[g]
            a = jnp.exp2(log_beta + within + carry)
            if diag:
                a = jnp.where(strict, a, 0.0)
            acc = acc + _dot(a.astype(BF16), v_ref[rows, heads[g]])
            out.append((carry + row_sum, acc))
        return tuple(out)

    def any_live(state):
        top = state[0][0]
        for g in range(1, ng):
            top = jnp.maximum(top, state[g][0])
        return jnp.max(top) > SB_DEAD_LOG2

    init = tuple((jnp.zeros((tq, 1), F32), jnp.zeros((tq, hd), F32)) for _ in range(ng))
    state = tile(qi, init, True)

    def cond(loop):
        i, live, _ = loop
        return jnp.logical_and(i < qi, live)

    def body(loop):
        i, _, st = loop
        st = tile(qi - 1 - i, st, False)
        return i + 1, any_live(st), st

    _, _, state = lax.while_loop(cond, body, (jnp.int32(0), any_live(state), state))
    for g in range(ng):
        o_ref[:, g * hd:(g + 1) * hd] = state[g][1].astype(o_ref.dtype)


def _sb_core(qkv, bsz, seq, hd, tq=256, heads_per_step=4):
    t = qkv.shape[0]
    nh = SB_HEADS
    tq = min(tq, seq)
    nq = seq // tq
    ngrp = nh // heads_per_step
    gw = heads_per_step * hd
    return pl.pallas_call(
        functools.partial(_sb_kernel, scale=hd ** -0.5, hd=hd),
        grid=(bsz, ngrp, nq),
        in_specs=[
            pl.BlockSpec((tq, gw), lambda b, h, i: (b * nq + i, h)),
            pl.BlockSpec((seq, gw), lambda b, h, i: (b, ngrp + h)),
            pl.BlockSpec((seq, gw), lambda b, h, i: (b, 2 * ngrp + h)),
        ],
        out_specs=pl.BlockSpec((tq, gw), lambda b, h, i: (b * nq + i, h)),
        out_shape=jax.ShapeDtypeStruct((t, nh * hd), BF16),
        compiler_params=_params("parallel", "parallel", "arbitrary"),
        name="sb_core",
    )(qkv, qkv, qkv)


def _pool_kernel(x_ref, u_ref, halo_ref, gate_ref, wg_ref, sc_ref, wo_ref, o_ref,
                 ext_ref, y_ref, *, tiles_per_seq):
    tm, d = u_ref.shape
    ngrp = len(POOL_WINDOWS)
    gw = d // ngrp
    tile_in_seq = pl.program_id(0) % tiles_per_seq
    ext_ref[0:POOL_HALO, :] = jnp.where(tile_in_seq == 0, 0.0, halo_ref[...])
    ext_ref[POOL_HALO:, :] = u_ref[...]
    pos = (tile_in_seq * tm + 1 + lax.broadcasted_iota(jnp.int32, (tm, 1), 0)).astype(F32)

    for gi, w in enumerate(POOL_WINDOWS):
        inv = 1.0 / jnp.minimum(pos, float(w))
        for cb in range(gw // 128):
            lo = gi * gw + cb * 128
            cols = slice(lo, lo + 128)
            ssum = ext_ref[:, cols]
            shift = 1
            while shift < w:
                ssum = ssum + pltpu.roll(ssum, shift, axis=0)
                shift *= 2
            uf = u_ref[:, cols]
            p = ssum[POOL_HALO:, :] * inv - uf
            y_ref[:, cols] = p.astype(y_ref.dtype)

    for gi in range(ngrp):
        cols = slice(gi * gw, (gi + 1) * gw)
        yg = _dot(y_ref[:, cols], wg_ref[gi]) * sc_ref[:, cols]
        y_ref[:, cols] = yg.astype(y_ref.dtype)

    _gated_residual_cols(x_ref, y_ref[...], gate_ref, wo_ref, o_ref)


def _pool_mix(x, u, mod, mod_base, w_grp, ch_scale, w_out, seq, tm=256):
    t, d = x.shape
    tm = min(tm, seq)
    tps = seq // tm
    hb = tm // POOL_HALO
    ngrp, gw, _ = w_grp.shape
    single = pl.Buffered(1)
    return pl.pallas_call(
        functools.partial(_pool_kernel, tiles_per_seq=tps),
        grid=(t // tm,),
        in_specs=[
            pl.BlockSpec((tm, d), lambda m: (m, 0)),
            pl.BlockSpec((tm, d), lambda m: (m, 0)),
            pl.BlockSpec((POOL_HALO, d), lambda m: (jnp.maximum(m * hb - 1, 0), 0)),
            _mod_spec(d, mod_base, 2, tps),
            pl.BlockSpec((ngrp, gw, gw), lambda m: (0, 0, 0), pipeline_mode=single),
            pl.BlockSpec((1, d), lambda m: (0, 0)),
            pl.BlockSpec((d, d), lambda m: (0, 0), pipeline_mode=single),
        ],
        out_specs=pl.BlockSpec((tm, d), lambda m: (m, 0)),
        out_shape=jax.ShapeDtypeStruct((t, d), F32),
        scratch_shapes=[pltpu.VMEM((tm + POOL_HALO, d), F32), pltpu.VMEM((tm, d), BF16)],
        compiler_params=_params("parallel"),
        name="pool_mix",
    )(x, u, u, mod, w_grp, ch_scale, w_out)


def kernel(x, c, ada_w, ada_b, norm_g, ffn_w13, ffn_w2, conv_w_in, conv_dw, conv_dw_b, conv_ln_g, conv_ln_b, conv_w_out, m_w_in, m_b_gate, m_conv_w, m_conv_b, m_norm_g, m_w_out, sb_w_in, sb_w_out, p_w_in, p_w_grp, p_scale, p_w_out, final_g):
    bsz, seq, d = x.shape
    depth = ada_w.shape[0]
    n_mixers = 4
    t = bsz * seq
    row = lambda a: a.reshape(1, -1)

    mod = _ada_mod(c, ada_w, ada_b)
    xt = x.reshape(t, d)
    w13 = ffn_w13[0, 0].astype(BF16)
    w2 = ffn_w2[0, 0].astype(BF16)

    for i in range(depth):
        mix, j = i % n_mixers, i // n_mixers
        base = i * bsz * N_MOD

        xt, w13, w2 = _ffn(xt, mod, base, row(norm_g[i, 0]), w13, w2, seq,
                           next_w=(ffn_w13, ffn_w2, i, 1))

        mbase = base + 3
        g1 = row(norm_g[i, 1])
        if mix == 0:
            u = _proj(xt, mod, mbase, g1, conv_w_in[j].astype(BF16), seq, F32, glu=True)
            u = _conv_core(u, conv_dw[j], row(conv_dw_b[j]), row(conv_ln_g[j]),
                           row(conv_ln_b[j]), seq)
            xt = _outproj(xt, u, mod, mbase, conv_w_out[j].astype(BF16), seq)
        elif mix == 1:
            dv = d // M_HEADS
            dk = (m_w_in.shape[2] - 2 * d - 2 * M_HEADS) // (2 * M_HEADS)
            wide = 2 * M_HEADS * dk + 2 * d
            w_in = m_w_in[j].astype(BF16)
            proj, gates, gates_t = _proj(xt, mod, mbase, g1, w_in[:, :wide], seq, F32,
                                         gate_w=w_in[:, wide:])
            u = _mlstm_core(proj, gates, gates_t, m_b_gate[j], m_conv_w[j],
                            row(m_conv_b[j]), row(m_norm_g[j]), bsz, seq, dk, dv)
            xt = _outproj(xt, u, mod, mbase, m_w_out[j].astype(BF16), seq)
        elif mix == 2:
            qkv = _proj(xt, mod, mbase, g1, sb_w_in[j].astype(BF16), seq, BF16)
            u = _sb_core(qkv, bsz, seq, d // SB_HEADS)
            xt = _outproj(xt, u, mod, mbase, sb_w_out[j].astype(BF16), seq)
        else:
            u = _proj(xt, mod, mbase, g1, p_w_in[j].astype(BF16), seq, F32)
            xt = _pool_mix(xt, u, mod, mbase, p_w_grp[j].astype(BF16), row(p_scale[j]),
                           p_w_out[j].astype(BF16), seq)

        g2 = row(norm_g[i, 2])
        if i + 1 < depth:
            xt, w13, w2 = _ffn(xt, mod, base + 6, g2, w13, w2, seq,
                               next_w=(ffn_w13, ffn_w2, i + 1, 0))
        else:
            xt = _ffn(xt, mod, base + 6, g2, w13, w2, seq, final_g=row(final_g))

    return xt.reshape(bsz, seq, d)
```

```python
import functools

import jax
import jax.numpy as jnp
from jax import lax
from jax.experimental import pallas as pl
from jax.experimental.pallas import tpu as pltpu

F32 = jnp.float32
BF16 = jnp.bfloat16

EPS = 1e-6
LN_EPS = 1e-5
LOG2E = 1.4426950408889634
N_SUB = 3
N_MOD = 3 * N_SUB
CONV_WIDTH = 31
CONV_HALO = 32
CONV_TM = 512
M_HEADS = 8
M_CONV_WIDTH = 4
M_CHUNK = 128
SB_HEADS = 16
SB_DEAD_LOG2 = -160.0
POOL_WINDOWS = (2, 4, 8, 16)
POOL_HALO = 16

VMEM_LIMIT_BYTES = 60 * 1024 * 1024
NORM_ROWS = 32
FIRST_STEP_ROWS = 256
ROW_CHUNK = 256
COL_CHUNK = 512


def _params(*semantics):
    return pltpu.CompilerParams(dimension_semantics=semantics,
                                vmem_limit_bytes=VMEM_LIMIT_BYTES)


def _dot(a, b):
    return jnp.dot(a, b, preferred_element_type=F32)


def _dot_nt(a, b):
    return lax.dot_general(a, b, (((1,), (1,)), ((), ())), preferred_element_type=F32)


def _split_dot(x, w, rhs=True):
    hi = x.astype(BF16)
    lo = (x - hi.astype(F32)).astype(BF16)
    if rhs:
        return _dot(hi, w) + _dot(lo, w)
    return _dot(w, hi) + _dot(w, lo)


def _norm_mod_rows(x_ref, g_ref, shift_ref, scale_ref, h_ref, start=None, size=None):
    rc = NORM_ROWS
    gain = g_ref[...] * (1.0 + scale_ref[...])
    shift = shift_ref[...]

    def chunk(rows):
        xf = x_ref[rows, :]
        inv = lax.rsqrt(jnp.mean(xf * xf, axis=-1, keepdims=True) + EPS)
        h_ref[rows, :] = (x_ref[rows, :] * inv * gain + shift).astype(h_ref.dtype)

    if start is not None:
        for r in range(size // rc):
            chunk(slice(start + r * rc, start + (r + 1) * rc))
        return

    def body(r, carry):
        chunk(pl.ds(pl.multiple_of(r * rc, rc), rc))
        return carry

    lax.fori_loop(0, x_ref.shape[0] // rc, body, 0, unroll=2)


def _mod_spec(d, base, j, tiles_per_seq):
    return pl.BlockSpec(
        (None, 1, d), lambda m, *_: (base + (m // tiles_per_seq) * N_MOD + j, 0, 0))


def _ada_kernel(c_ref, w_ref, b_ref, o_ref):
    c_act = jax.nn.silu(c_ref[...]).astype(BF16)
    o_ref[...] = _dot(c_act, w_ref[...].astype(BF16)) + b_ref[...]


def _ada_mod(c, ada_w, ada_b, tn=1024):
    depth, d, n = ada_w.shape
    bsz = c.shape[0]
    tn = min(tn, n)
    out = pl.pallas_call(
        _ada_kernel,
        grid=(depth, n // tn),
        in_specs=[
            pl.BlockSpec((bsz, d), lambda i, j: (0, 0)),
            pl.BlockSpec((None, d, tn), lambda i, j: (i, 0, j)),
            pl.BlockSpec((None, 1, tn), lambda i, j: (i, 0, j)),
        ],
        out_specs=pl.BlockSpec((None, bsz, tn), lambda i, j: (i, 0, j)),
        out_shape=jax.ShapeDtypeStruct((depth, bsz, n), F32),
        compiler_params=_params("parallel", "parallel"),
        name="ada_mod",
    )(c, ada_w, ada_b.reshape(depth, 1, n))
    return out.reshape(depth * bsz * N_MOD, 1, d)


def _ffn_kernel(*refs, final_norm, cast_next):
    x_ref, shift_ref, scale_ref, gate_ref, g_ref, wa_ref, wb_ref, w2_ref = refs[:8]
    rest = list(refs[8:])
    fg_ref = rest.pop(0) if final_norm else None
    next_in = (rest.pop(0), rest.pop(0)) if cast_next else ()
    o_ref = rest.pop(0)
    next_out = (rest.pop(0), rest.pop(0)) if cast_next else ()
    (h_ref,) = rest

    f = pl.program_id(1)
    tm, d = x_ref.shape
    cc = min(COL_CHUNK, d)

    for src, dst in zip(next_in, next_out):
        dst[...] = src[...].astype(dst.dtype)

    def swiglu_rows(rows, first):
        h = h_ref[rows, :]
        a = _dot(h, wa_ref[...])
        b = _dot(h, wb_ref[...])
        act = (jax.nn.silu(a) * b).astype(BF16)
        for n in range(d // cc):
            cols = slice(n * cc, (n + 1) * cc)
            y = _dot(act, w2_ref[:, cols])
            if first:
                o_ref[rows, cols] = y
            else:
                o_ref[rows, cols] += y

    @pl.when(f == 0)
    def _():
        fc = min(FIRST_STEP_ROWS, tm)
        for c in range(tm // fc):
            _norm_mod_rows(x_ref, g_ref, shift_ref, scale_ref, h_ref, c * fc, fc)
            swiglu_rows(slice(c * fc, (c + 1) * fc), True)

    @pl.when(f > 0)
    def _():
        swiglu_rows(slice(None), False)

    @pl.when(f == pl.num_programs(1) - 1)
    def _():
        half_gate = 0.5 * gate_ref[...]
        rc = NORM_ROWS if final_norm else min(ROW_CHUNK, tm)

        def body(r, carry):
            rows = pl.ds(pl.multiple_of(r * rc, rc), rc)
            y = x_ref[rows, :] + half_gate * o_ref[rows, :]
            if final_norm:
                y = y * lax.rsqrt(jnp.mean(y * y, axis=-1, keepdims=True) + EPS) * fg_ref[...]
            o_ref[rows, :] = y
            return carry

        lax.fori_loop(0, tm // rc, body, 0)


def _ffn(x, mod, mod_base, norm_g, w13, w2, seq, next_w=None, final_g=None, tm=1024, tf=512):
    t, d = x.shape
    ff = w2.shape[0]
    tm = min(tm, seq)
    tf = min(tf, ff)
    tps = seq // tm
    nm = t // tm
    nf = ff // tf
    in_specs = [
        pl.BlockSpec((tm, d), lambda m, f: (m, 0)),
        _mod_spec(d, mod_base, 0, tps),
        _mod_spec(d, mod_base, 1, tps),
        _mod_spec(d, mod_base, 2, tps),
        pl.BlockSpec((1, d), lambda m, f: (0, 0)),
        pl.BlockSpec((d, tf), lambda m, f: (0, f)),
        pl.BlockSpec((d, tf), lambda m, f: (0, nf + f)),
        pl.BlockSpec((tf, d), lambda m, f: (f, 0)),
    ]
    args = [x, mod, mod, mod, norm_g, w13, w13, w2]
    out_specs = [pl.BlockSpec((tm, d), lambda m, f: (m, 0))]
    out_shape = [jax.ShapeDtypeStruct((t, d), F32)]
    if final_g is not None:
        in_specs.append(pl.BlockSpec((1, d), lambda m, f: (0, 0)))
        args.append(final_g)
    if next_w is not None:
        stack13, stack2, layer, which = next_w
        r13, c13 = d // nm, 2 * ff // nf
        r2, c2 = ff // nf, d // nm
        in_specs += [
            pl.BlockSpec((None, None, r13, c13), lambda m, f: (layer, which, m, f)),
            pl.BlockSpec((None, None, r2, c2), lambda m, f: (layer, which, f, m)),
        ]
        args += [stack13, stack2]
        out_specs += [pl.BlockSpec((r13, c13), lambda m, f: (m, f)),
                      pl.BlockSpec((r2, c2), lambda m, f: (f, m))]
        out_shape += [jax.ShapeDtypeStruct((d, 2 * ff), BF16),
                      jax.ShapeDtypeStruct((ff, d), BF16)]
    out = pl.pallas_call(
        functools.partial(_ffn_kernel, final_norm=final_g is not None,
                          cast_next=next_w is not None),
        grid=(nm, nf),
        in_specs=in_specs,
        out_specs=out_specs,
        out_shape=out_shape,
        scratch_shapes=[pltpu.VMEM((tm, d), BF16)],
        compiler_params=_params("parallel", "arbitrary"),
        name="ffn",
    )(*args)
    return out if next_w is not None else out[0]


def _proj_steps(x_ref, shift_ref, scale_ref, g_ref, h_ref, emit, emit_first=None):
    tm = x_ref.shape[0]

    @pl.when(pl.program_id(1) == 0)
    def _():
        fc = min(FIRST_STEP_ROWS, tm)
        for c in range(tm // fc):
            rows = slice(c * fc, (c + 1) * fc)
            _norm_mod_rows(x_ref, g_ref, shift_ref, scale_ref, h_ref, c * fc, fc)
            emit(rows)
            if emit_first is not None:
                emit_first(rows)

    @pl.when(pl.program_id(1) > 0)
    def _():
        emit(slice(None))


def _proj_kernel(x_ref, shift_ref, scale_ref, g_ref, w_ref, o_ref, h_ref):
    tn = o_ref.shape[1]
    cc = min(2 * COL_CHUNK, tn)

    def emit(rows):
        h = h_ref[rows, :]
        for n in range(tn // cc):
            cols = slice(n * cc, (n + 1) * cc)
            o_ref[rows, cols] = _dot(h, w_ref[:, cols]).astype(o_ref.dtype)

    _proj_steps(x_ref, shift_ref, scale_ref, g_ref, h_ref, emit)


def _proj_glu_kernel(x_ref, shift_ref, scale_ref, g_ref, wa_ref, wg_ref, o_ref, h_ref):
    def emit(rows):
        h = h_ref[rows, :]
        a = _dot(h, wa_ref[...])
        g = _dot(h, wg_ref[...])
        o_ref[rows, :] = (a * jax.nn.sigmoid(g)).astype(o_ref.dtype)

    _proj_steps(x_ref, shift_ref, scale_ref, g_ref, h_ref, emit)


def _proj_gates_kernel(x_ref, shift_ref, scale_ref, g_ref, w_ref, wg_ref, wgt_ref,
                       o_ref, gates_ref, gates_t_ref, h_ref):
    def emit(rows):
        o_ref[rows, :] = _dot(h_ref[rows, :], w_ref[...]).astype(o_ref.dtype)

    def emit_gates(rows):
        h = h_ref[rows, :]
        gates_ref[rows, :] = _dot(h, wg_ref[...])
        gates_t_ref[:, rows] = _dot_nt(wgt_ref[...], h)

    _proj_steps(x_ref, shift_ref, scale_ref, g_ref, h_ref, emit, emit_gates)


def _proj(x, mod, mod_base, norm_g, w, seq, out_dtype, tm=1024, tn=1024, glu=False,
          gate_w=None):
    t, d = x.shape
    n = w.shape[1] // 2 if glu else w.shape[1]
    if gate_w is not None:
        n -= gate_w.shape[1]
    tm = min(tm, seq)
    tn = min(tn // 2 if glu else tn, n)
    tps = seq // tm
    nn = n // tn
    x_spec = pl.BlockSpec((tm, d), lambda m, j: (m, 0))
    common = [x_spec, _mod_spec(d, mod_base, 0, tps), _mod_spec(d, mod_base, 1, tps),
              pl.BlockSpec((1, d), lambda m, j: (0, 0))]
    out_spec = pl.BlockSpec((tm, tn), lambda m, j: (m, j))
    out_shape = jax.ShapeDtypeStruct((t, n), out_dtype)
    scratch = [pltpu.VMEM((tm, d), BF16)]
    cp = _params("parallel", "arbitrary")
    if glu:
        return pl.pallas_call(
            _proj_glu_kernel, grid=(t // tm, nn),
            in_specs=common + [pl.BlockSpec((d, tn), lambda m, j: (0, j)),
                               pl.BlockSpec((d, tn), lambda m, j: (0, nn + j))],
            out_specs=out_spec, out_shape=out_shape, scratch_shapes=scratch,
            compiler_params=cp, name="proj_glu",
        )(x, mod, mod, norm_g, w, w)
    if gate_w is not None:
        ng = gate_w.shape[1]
        return pl.pallas_call(
            _proj_gates_kernel, grid=(t // tm, nn),
            in_specs=common + [pl.BlockSpec((d, tn), lambda m, j: (0, j)),
                               pl.BlockSpec((d, ng), lambda m, j: (0, 0)),
                               pl.BlockSpec((ng, d), lambda m, j: (0, 0))],
            out_specs=[out_spec,
                       pl.BlockSpec((tm, ng), lambda m, j: (m, 0)),
                       pl.BlockSpec((ng, tm), lambda m, j: (0, m))],
            out_shape=[out_shape,
                       jax.ShapeDtypeStruct((t, ng), F32),
                       jax.ShapeDtypeStruct((ng, t), F32)],
            scratch_shapes=scratch, compiler_params=cp, name="proj_gates",
        )(x, mod, mod, norm_g, w, gate_w, gate_w.T)
    return pl.pallas_call(
        _proj_kernel, grid=(t // tm, nn),
        in_specs=common + [pl.BlockSpec((d, tn), lambda m, j: (0, j))],
        out_specs=out_spec, out_shape=out_shape, scratch_shapes=scratch,
        compiler_params=cp, name="proj",
    )(x, mod, mod, norm_g, w)


def _gated_residual_cols(x_ref, u, gate_ref, w_ref, o_ref):
    d = x_ref.shape[1]
    cc = min(COL_CHUNK, d)
    for n in range(d // cc):
        cols = slice(n * cc, (n + 1) * cc)
        o_ref[:, cols] = x_ref[:, cols] + gate_ref[:, cols] * _dot(u, w_ref[:, cols])


def _outproj_kernel(x_ref, u_ref, gate_ref, w_ref, o_ref):
    _gated_residual_cols(x_ref, u_ref[...], gate_ref, w_ref, o_ref)


def _outproj(x, u, mod, mod_base, w, seq, tm=1024):
    t, d = x.shape
    k = u.shape[1]
    tm = min(tm, seq)
    tps = seq // tm
    return pl.pallas_call(
        _outproj_kernel,
        grid=(t // tm,),
        in_specs=[
            pl.BlockSpec((tm, d), lambda m: (m, 0)),
            pl.BlockSpec((tm, k), lambda m: (m, 0)),
            _mod_spec(d, mod_base, 2, tps),
            pl.BlockSpec((k, d), lambda m: (0, 0), pipeline_mode=pl.Buffered(1)),
        ],
        out_specs=pl.BlockSpec((tm, d), lambda m: (m, 0)),
        out_shape=jax.ShapeDtypeStruct((t, d), F32),
        compiler_params=_params("parallel"),
        name="outproj",
    )(x, u, mod, w)


def _conv_core_kernel(*refs, tiles_per_seq, n_casts):
    u_ref, halo_ref, dw_ref, dwb_ref, lng_ref, lnb_ref = refs[:6]
    cast_in = refs[6:6 + n_casts]
    o_ref = refs[6 + n_casts]
    cast_out = refs[7 + n_casts:7 + 2 * n_casts]
    ext_ref, y_ref = refs[7 + 2 * n_casts:]

    for src, dst in zip(cast_in, cast_out):
        dst[...] = src[...].astype(dst.dtype)

    tm, d = u_ref.shape
    first = (pl.program_id(0) % tiles_per_seq) == 0
    ext_ref[0:CONV_HALO, :] = jnp.where(first, 0.0, halo_ref[...])
    ext_ref[CONV_HALO:, :] = u_ref[...]

    rc = 64
    lead = CONV_HALO - (CONV_WIDTH - 1)
    wrows = rc + CONV_HALO
    for cb in range(d // 128):
        cols = slice(cb * 128, (cb + 1) * 128)
        bias = dwb_ref[:, cols]

        def body(r, carry, cols=cols, bias=bias):
            base = pl.multiple_of(r * rc, rc)
            win = ext_ref[pl.ds(base, wrows), cols]
            acc = jnp.broadcast_to(bias, (rc, 128))
            for res in range(8):
                shifted = pltpu.roll(win, wrows - res, axis=0) if res else win
                for k in range(CONV_WIDTH):
                    if (lead + k) % 8 == res:
                        start = lead + k - res
                        acc = acc + shifted[start:start + rc, :] * dw_ref[k:k + 1, cols]
            y_ref[pl.ds(base, rc), cols] = acc
            return carry

        lax.fori_loop(0, tm // rc, body, 0)

    lrc = min(ROW_CHUNK, tm)
    ln_g = lng_ref[...]
    ln_b = lnb_ref[...]

    def ln_body(r, carry):
        rows = pl.ds(pl.multiple_of(r * lrc, lrc), lrc)
        y = y_ref[rows, :]
        mu = jnp.mean(y, axis=-1, keepdims=True)
        yc = y - mu
        var = jnp.mean(yc * yc, axis=-1, keepdims=True)
        z = yc * lax.rsqrt(var + LN_EPS) * ln_g + ln_b
        o_ref[rows, :] = jax.nn.silu(z).astype(o_ref.dtype)
        return carry

    lax.fori_loop(0, tm // lrc, ln_body, 0)


def _conv_core(u, dw, dw_b, ln_g, ln_b, seq, casts=(), tm=CONV_TM):
    t, d = u.shape
    tm = min(tm, seq)
    tps = seq // tm
    hb = tm // CONV_HALO
    nm = t // tm
    cast_specs = [pl.BlockSpec((w.shape[0] // nm, w.shape[1]), lambda m: (m, 0)) for w in casts]
    out = pl.pallas_call(
        functools.partial(_conv_core_kernel, tiles_per_seq=tps, n_casts=len(casts)),
        grid=(nm,),
        in_specs=[
            pl.BlockSpec((tm, d), lambda m: (m, 0)),
            pl.BlockSpec((CONV_HALO, d), lambda m: (jnp.maximum(m * hb - 1, 0), 0)),
            pl.BlockSpec((CONV_WIDTH, d), lambda m: (0, 0)),
            pl.BlockSpec((1, d), lambda m: (0, 0)),
            pl.BlockSpec((1, d), lambda m: (0, 0)),
            pl.BlockSpec((1, d), lambda m: (0, 0)),
        ] + cast_specs,
        out_specs=[pl.BlockSpec((tm, d), lambda m: (m, 0))] + cast_specs,
        out_shape=[jax.ShapeDtypeStruct((t, d), BF16)]
        + [jax.ShapeDtypeStruct(w.shape, BF16) for w in casts],
        scratch_shapes=[pltpu.VMEM((tm + CONV_HALO, d), F32), pltpu.VMEM((tm, d), F32)],
        compiler_params=_params("parallel"),
        name="conv_core",
    )(u, u, dw, dw_b, ln_g, ln_b, *casts)
    return out[0], out[1:]


def _mlstm_kernel(qk_ref, v_ref, o_ref, gates_ref, gates_t_ref, bg_ref, bgt_ref,
                  cw_ref, cb_ref, ng_ref, out_ref,
                  c_state, n_state, m_state, win_ref, *, dk, dv):
    lc = M_CHUNK
    nh = M_HEADS
    hist = 8

    @pl.when(pl.program_id(1) == 0)
    def _():
        c_state[...] = jnp.zeros_like(c_state)
        n_state[...] = jnp.zeros_like(n_state)
        m_state[...] = jnp.zeros_like(m_state)
        win_ref[0:hist, :] = jnp.zeros((hist, win_ref.shape[1]), F32)

    win_ref[hist:, :] = qk_ref[...]

    row = lax.broadcasted_iota(jnp.int32, (lc, lc), 0)
    col = lax.broadcasted_iota(jnp.int32, (lc, lc), 1)
    causal = col <= row
    tri = causal.astype(BF16)
    tri_t = (row <= col).astype(BF16)

    g_col = gates_ref[...] + bg_ref[...]
    g_row = gates_t_ref[...] + bgt_ref[...]
    li_col = g_col[:, :nh]
    li_row = g_row[:nh, :]
    lf_col = jax.nn.log_sigmoid(g_col[:, nh:])
    lf_row = jax.nn.log_sigmoid(g_row[nh:, :])
    b_col = _split_dot(lf_col, tri, rhs=False)
    b_row = _split_dot(lf_row, tri_t)

    qscale = dk ** -0.5

    def conv_silu(cols):
        acc = jnp.broadcast_to(cb_ref[:, cols], (lc, dk))
        for k in range(M_CONV_WIDTH):
            start = hist - (M_CONV_WIDTH - 1) + k
            acc = acc + win_ref[start:start + lc, cols] * cw_ref[k:k + 1, cols]
        return jax.nn.silu(acc)

    first = []
    for h in range(nh):
        q = conv_silu(slice(h * dk, (h + 1) * dk)) * qscale
        k = conv_silu(slice((nh + h) * dk, (nh + h + 1) * dk))
        q16 = q.astype(BF16)
        v16 = v_ref[:, h * dv:(h + 1) * dv].astype(BF16)
        bc = b_col[:, h:h + 1]
        br = b_row[h:h + 1, :]
        m_prev = m_state[h:h + 1, 0:1]
        b_last = br[:, lc - 1:lc]
        g = b_last - bc + li_col[:, h:h + 1]
        m_new = jnp.maximum(b_last + m_prev, jnp.max(g, axis=0, keepdims=True))
        decay = jnp.exp(b_last + m_prev - m_new)
        kw = k * jnp.exp(g - m_new)
        first.append(dict(
            v16=v16, bc=bc, br=br, m_prev=m_prev, m_new=m_new, decay=decay,
            qk=_dot_nt(q16, k.astype(BF16)),
            qc=_dot(q16, c_state[h].astype(BF16)),
            kv=_dot(kw.T.astype(BF16), v16),
            qn=jnp.sum(q * n_state[h:h + 1, :], axis=-1, keepdims=True),
            kw_sum=jnp.sum(kw, axis=0, keepdims=True)))

    second = []
    for h in range(nh):
        f = first[h]
        dmat = jnp.where(causal, f["bc"] - f["br"] + li_row[h:h + 1, :], -jnp.inf)
        inter = f["bc"] + f["m_prev"]
        m_t = jnp.maximum(inter, jnp.max(dmat, axis=-1, keepdims=True))
        w_inter = jnp.exp(inter - m_t)
        s = f["qk"] * jnp.exp(dmat - m_t)
        second.append(dict(
            m_t=m_t, w_inter=w_inter, s_sum=jnp.sum(s, axis=-1, keepdims=True),
            sv=_dot(s.astype(BF16), f["v16"])))

    for h in range(nh):
        f, s2 = first[h], second[h]
        vcols = slice(h * dv, (h + 1) * dv)
        num = s2["w_inter"] * f["qc"] + s2["sv"]
        den = s2["w_inter"] * f["qn"] + s2["s_sum"]
        hb = num / jnp.maximum(jnp.abs(den), jnp.exp(-s2["m_t"]))
        hn = hb * lax.rsqrt(jnp.mean(hb * hb, axis=-1, keepdims=True) + EPS)
        hn = hn * ng_ref[:, vcols]
        out_ref[:, vcols] = (hn * jax.nn.sigmoid(o_ref[:, vcols])).astype(out_ref.dtype)
        c_state[h] = f["decay"] * c_state[h] + f["kv"]
        n_state[h:h + 1, :] = f["decay"] * n_state[h:h + 1, :] + f["kw_sum"]
        m_state[h:h + 1, :] = jnp.broadcast_to(f["m_new"], (1, m_state.shape[1]))

    win_ref[0:hist, :] = qk_ref[lc - hist:lc, :]


def _mlstm_core(proj, gates, gates_t, b_gate, conv_w, conv_b, norm_g, bsz, seq, dk, dv):
    t = proj.shape[0]
    nh = M_HEADS
    nc = seq // M_CHUNK
    qkw = 2 * nh * dk
    vw = nh * dv
    assert qkw == vw, "projection column blocks assume equal q|k and v widths"
    ng = 2 * nh
    const = lambda r, c: pl.BlockSpec((r, c), lambda b, i: (0, 0))
    return pl.pallas_call(
        functools.partial(_mlstm_kernel, dk=dk, dv=dv),
        grid=(bsz, nc),
        in_specs=[
            pl.BlockSpec((M_CHUNK, qkw), lambda b, i: (b * nc + i, 0)),
            pl.BlockSpec((M_CHUNK, vw), lambda b, i: (b * nc + i, 1)),
            pl.BlockSpec((M_CHUNK, vw), lambda b, i: (b * nc + i, 2)),
            pl.BlockSpec((M_CHUNK, ng), lambda b, i: (b * nc + i, 0)),
            pl.BlockSpec((ng, M_CHUNK), lambda b, i: (0, b * nc + i)),
            const(1, ng), const(ng, 1),
            const(M_CONV_WIDTH, qkw), const(1, qkw), const(1, vw),
        ],
        out_specs=pl.BlockSpec((M_CHUNK, vw), lambda b, i: (b * nc + i, 0)),
        out_shape=jax.ShapeDtypeStruct((t, vw), BF16),
        scratch_shapes=[
            pltpu.VMEM((nh, dk, dv), F32),
            pltpu.VMEM((nh, dk), F32),
            pltpu.VMEM((nh, 128), F32),
            pltpu.VMEM((M_CHUNK + 8, qkw), F32),
        ],
        compiler_params=_params("parallel", "arbitrary"),
        name="mlstm_core",
    )(proj, proj, proj, gates, gates_t, b_gate.reshape(1, ng), b_gate.reshape(ng, 1),
      conv_w, conv_b, norm_g)


def _sb_kernel(q_ref, k_ref, v_ref, o_ref, *, scale, hd):
    tq = q_ref.shape[0]
    ng = q_ref.shape[1] // hd
    qi = pl.program_id(2)
    row = lax.broadcasted_iota(jnp.int32, (tq, tq), 0)
    col = lax.broadcasted_iota(jnp.int32, (tq, tq), 1)
    strict = col < row
    after = (row > col).astype(BF16)
    after2 = jnp.concatenate([after, after], axis=0)

    def tile(kj, state, diag):
        rows = pl.ds(pl.multiple_of(kj * tq, tq), tq)
        heads = [slice(g * hd, (g + 1) * hd) for g in range(ng)]
        zs = [_dot_nt(q_ref[:, cols], k_ref[rows, cols]) * (scale * LOG2E) for cols in heads]
        mids = []
        for g in range(ng):
            z = zs[g]
            log_beta = jnp.minimum(z, 0.0) - jnp.log2(1.0 + jnp.exp2(-jnp.abs(z)))
            log_keep = log_beta - z
            if diag:
                log_keep = jnp.where(strict, log_keep, 0.0)
            hi = log_keep.astype(BF16)
            lo = (log_keep - hi.astype(F32)).astype(BF16)
            within = _dot(jnp.concatenate([hi, lo], axis=1), after2)
            mids.append((log_beta, within, within[:, 0:1] + log_keep[:, 0:1]))
        out = []
        for g in range(ng):
            carry, acc = state[g]
            log_beta, within, row_sum = mids[g]
            a = jnp.exp2(log_beta + within + carry)
            if diag:
                a = jnp.where(strict, a, 0.0)
            acc = acc + _dot(a.astype(BF16), v_ref[rows, heads[g]])
            out.append((carry + row_sum, acc))
        return tuple(out)

    def any_live(state):
        top = state[0][0]
        for g in range(1, ng):
            top = jnp.maximum(top, state[g][0])
        return jnp.max(top) > SB_DEAD_LOG2

    init = tuple((jnp.zeros((tq, 1), F32), jnp.zeros((tq, hd), F32)) for _ in range(ng))
    state = tile(qi, init, True)

    def cond(loop):
        i, live, _ = loop
        return jnp.logical_and(i < qi, live)

    def body(loop):
        i, _, st = loop
        st = tile(qi - 1 - i, st, False)
        return i + 1, any_live(st), st

    _, _, state = lax.while_loop(cond, body, (jnp.int32(0), any_live(state), state))
    for g in range(ng):
        o_ref[:, g * hd:(g + 1) * hd] = state[g][1].astype(o_ref.dtype)


def _sb_core(qkv, bsz, seq, hd, tq=256, heads_per_step=8):
    t = qkv.shape[0]
    nh = SB_HEADS
    tq = min(tq, seq)
    nq = seq // tq
    ngrp = nh // heads_per_step
    gw = heads_per_step * hd
    return pl.pallas_call(
        functools.partial(_sb_kernel, scale=hd ** -0.5, hd=hd),
        grid=(bsz, ngrp, nq),
        in_specs=[
            pl.BlockSpec((tq, gw), lambda b, h, i: (b * nq + i, h)),
            pl.BlockSpec((seq, gw), lambda b, h, i: (b, ngrp + h)),
            pl.BlockSpec((seq, gw), lambda b, h, i: (b, 2 * ngrp + h)),
        ],
        out_specs=pl.BlockSpec((tq, gw), lambda b, h, i: (b * nq + i, h)),
        out_shape=jax.ShapeDtypeStruct((t, nh * hd), BF16),
        compiler_params=_params("parallel", "parallel", "arbitrary"),
        name="sb_core",
    )(qkv, qkv, qkv)


def _pool_kernel(x_ref, u_ref, halo_ref, gate_ref, wg_ref, sc_ref, wo_ref, o_ref,
                 ext_ref, y_ref, *, tiles_per_seq):
    tm, d = u_ref.shape
    ngrp = len(POOL_WINDOWS)
    gw = d // ngrp
    tile_in_seq = pl.program_id(0) % tiles_per_seq
    ext_ref[0:POOL_HALO, :] = jnp.where(tile_in_seq == 0, 0.0, halo_ref[...])
    ext_ref[POOL_HALO:, :] = u_ref[...]
    pos = (tile_in_seq * tm + 1 + lax.broadcasted_iota(jnp.int32, (tm, 1), 0)).astype(F32)

    for gi, w in enumerate(POOL_WINDOWS):
        inv = 1.0 / jnp.minimum(pos, float(w))
        for cb in range(gw // 128):
            lo = gi * gw + cb * 128
            cols = slice(lo, lo + 128)
            ssum = ext_ref[:, cols]
            shift = 1
            while shift < w:
                ssum = ssum + pltpu.roll(ssum, shift, axis=0)
                shift *= 2
            uf = u_ref[:, cols]
            p = ssum[POOL_HALO:, :] * inv - uf
            y_ref[:, cols] = p.astype(y_ref.dtype)

    for gi in range(ngrp):
        cols = slice(gi * gw, (gi + 1) * gw)
        yg = _dot(y_ref[:, cols], wg_ref[gi]) * sc_ref[:, cols]
        y_ref[:, cols] = yg.astype(y_ref.dtype)

    _gated_residual_cols(x_ref, y_ref[...], gate_ref, wo_ref, o_ref)


def _pool_mix(x, u, mod, mod_base, w_grp, ch_scale, w_out, seq, tm=512):
    t, d = x.shape
    tm = min(tm, seq)
    tps = seq // tm
    hb = tm // POOL_HALO
    ngrp, gw, _ = w_grp.shape
    single = pl.Buffered(1)
    return pl.pallas_call(
        functools.partial(_pool_kernel, tiles_per_seq=tps),
        grid=(t // tm,),
        in_specs=[
            pl.BlockSpec((tm, d), lambda m: (m, 0)),
            pl.BlockSpec((tm, d), lambda m: (m, 0)),
            pl.BlockSpec((POOL_HALO, d), lambda m: (jnp.maximum(m * hb - 1, 0), 0)),
            _mod_spec(d, mod_base, 2, tps),
            pl.BlockSpec((ngrp, gw, gw), lambda m: (0, 0, 0), pipeline_mode=single),
            pl.BlockSpec((1, d), lambda m: (0, 0)),
            pl.BlockSpec((d, d), lambda m: (0, 0), pipeline_mode=single),
        ],
        out_specs=pl.BlockSpec((tm, d), lambda m: (m, 0)),
        out_shape=jax.ShapeDtypeStruct((t, d), F32),
        scratch_shapes=[pltpu.VMEM((tm + POOL_HALO, d), F32), pltpu.VMEM((tm, d), BF16)],
        compiler_params=_params("parallel"),
        name="pool_mix",
    )(x, u, u, mod, w_grp, ch_scale, w_out)


def kernel(x, c, ada_w, ada_b, norm_g, ffn_w13, ffn_w2, conv_w_in, conv_dw, conv_dw_b, conv_ln_g, conv_ln_b, conv_w_out, m_w_in, m_b_gate, m_conv_w, m_conv_b, m_norm_g, m_w_out, sb_w_in, sb_w_out, p_w_in, p_w_grp, p_scale, p_w_out, final_g):
    bsz, seq, d = x.shape
    depth = ada_w.shape[0]
    n_mixers = 4
    t = bsz * seq
    row = lambda a: a.reshape(1, -1)

    mod = _ada_mod(c, ada_w, ada_b)
    xt = x.reshape(t, d)
    w13 = ffn_w13[0, 0].astype(BF16)
    w2 = ffn_w2[0, 0].astype(BF16)

    def mixer_mats(i):
        mix, j = i % n_mixers, i // n_mixers
        if mix == 0:
            return {"w_in": conv_w_in[j], "w_out": conv_w_out[j]}
        if mix == 1:
            return {"w_in": m_w_in[j], "w_out": m_w_out[j]}
        if mix == 2:
            return {"w_in": sb_w_in[j], "w_out": sb_w_out[j]}
        return {"w_in": p_w_in[j], "w_out": p_w_out[j],
                "w_grp": p_w_grp[j].reshape(-1, p_w_grp.shape[-1])}

    half = {}

    for i in range(depth):
        mix, j = i % n_mixers, i // n_mixers
        base = i * bsz * N_MOD

        xt, w13, w2 = _ffn(xt, mod, base, row(norm_g[i, 0]), w13, w2, seq,
                           next_w=(ffn_w13, ffn_w2, i, 1))

        mbase = base + 3
        g1 = row(norm_g[i, 1])
        wts = {name: half[(i, name)] if (i, name) in half else w.astype(BF16)
               for name, w in mixer_mats(i).items()}
        if mix == 0:
            u = _proj(xt, mod, mbase, g1, wts["w_in"], seq, F32, glu=True)
            steps = t // min(CONV_TM, seq)
            later = [(k, name, w) for k in range(i + 1, depth)
                     for name, w in mixer_mats(k).items()
                     if (k, name) not in half and w.shape[0] % (16 * steps) == 0]
            u, converted = _conv_core(u, conv_dw[j], row(conv_dw_b[j]), row(conv_ln_g[j]),
                                      row(conv_ln_b[j]), seq, casts=[w for _, _, w in later])
            half.update({(k, name): wb for (k, name, _), wb in zip(later, converted)})
            xt = _outproj(xt, u, mod, mbase, wts["w_out"], seq)
        elif mix == 1:
            dv = d // M_HEADS
            dk = (m_w_in.shape[2] - 2 * d - 2 * M_HEADS) // (2 * M_HEADS)
            wide = 2 * M_HEADS * dk + 2 * d
            proj, gates, gates_t = _proj(xt, mod, mbase, g1, wts["w_in"], seq, F32,
                                         gate_w=wts["w_in"][:, wide:])
            u = _mlstm_core(proj, gates, gates_t, m_b_gate[j], m_conv_w[j],
                            row(m_conv_b[j]), row(m_norm_g[j]), bsz, seq, dk, dv)
            xt = _outproj(xt, u, mod, mbase, wts["w_out"], seq)
        elif mix == 2:
            qkv = _proj(xt, mod, mbase, g1, wts["w_in"], seq, BF16, tn=2048)
            u = _sb_core(qkv, bsz, seq, d // SB_HEADS)
            xt = _outproj(xt, u, mod, mbase, wts["w_out"], seq)
        else:
            u = _proj(xt, mod, mbase, g1, wts["w_in"], seq, F32)
            xt = _pool_mix(xt, u, mod, mbase, wts["w_grp"].reshape(p_w_grp.shape[1:]),
                           row(p_scale[j]), wts["w_out"], seq)

        g2 = row(norm_g[i, 2])
        if i + 1 < depth:
            xt, w13, w2 = _ffn(xt, mod, base + 6, g2, w13, w2, seq,
                               next_w=(ffn_w13, ffn_w2, i + 1, 0))
        else:
            xt = _ffn(xt, mod, base + 6, g2, w13, w2, seq, final_g=row(final_g))

    return xt.reshape(bsz, seq, d)
```

```python
import functools

import jax
import jax.numpy as jnp
from jax import lax
from jax.experimental import pallas as pl
from jax.experimental.pallas import tpu as pltpu

F32 = jnp.float32
BF16 = jnp.bfloat16

EPS = 1e-6
LN_EPS = 1e-5
LOG2E = 1.4426950408889634
N_SUB = 3
N_MOD = 3 * N_SUB
CONV_WIDTH = 31
CONV_HALO = 32
CONV_TM = 512
M_HEADS = 8
M_CONV_WIDTH = 4
M_CHUNK = 128
SB_HEADS = 16
SB_DEAD_LOG2 = -160.0
POOL_WINDOWS = (2, 4, 8, 16)
POOL_HALO = 16

VMEM_LIMIT_BYTES = 60 * 1024 * 1024
NORM_ROWS = 32
FIRST_STEP_ROWS = 256
ROW_CHUNK = 256
COL_CHUNK = 512


def _params(*semantics):
    return pltpu.CompilerParams(dimension_semantics=semantics,
                                vmem_limit_bytes=VMEM_LIMIT_BYTES)


def _dot(a, b):
    return jnp.dot(a, b, preferred_element_type=F32)


def _dot_nt(a, b):
    return lax.dot_general(a, b, (((1,), (1,)), ((), ())), preferred_element_type=F32)


def _split_dot(x, w):
    hi = x.astype(BF16)
    lo = (x - hi.astype(F32)).astype(BF16)
    return _dot(hi, w) + _dot(lo, w)


def _norm_mod_rows(x_ref, g_ref, shift_ref, scale_ref, h_ref, start=None, size=None):
    rc = NORM_ROWS
    gain = g_ref[...] * (1.0 + scale_ref[...])
    shift = shift_ref[...]

    def chunk(rows):
        xf = x_ref[rows, :]
        inv = lax.rsqrt(jnp.mean(xf * xf, axis=-1, keepdims=True) + EPS)
        h_ref[rows, :] = (x_ref[rows, :] * inv * gain + shift).astype(h_ref.dtype)

    if start is not None:
        for r in range(size // rc):
            chunk(slice(start + r * rc, start + (r + 1) * rc))
        return

    def body(r, carry):
        chunk(pl.ds(pl.multiple_of(r * rc, rc), rc))
        return carry

    lax.fori_loop(0, x_ref.shape[0] // rc, body, 0, unroll=2)


def _mod_spec(d, base, j, tiles_per_seq):
    return pl.BlockSpec(
        (None, 1, d), lambda m, *_: (base + (m // tiles_per_seq) * N_MOD + j, 0, 0))


def _ada_kernel(c_ref, w_ref, b_ref, o_ref):
    c_act = jax.nn.silu(c_ref[...]).astype(BF16)
    o_ref[...] = _dot(c_act, w_ref[...].astype(BF16)) + b_ref[...]


def _ada_mod(c, ada_w, ada_b, tn=1024):
    depth, d, n = ada_w.shape
    bsz = c.shape[0]
    tn = min(tn, n)
    out = pl.pallas_call(
        _ada_kernel,
        grid=(depth, n // tn),
        in_specs=[
            pl.BlockSpec((bsz, d), lambda i, j: (0, 0)),
            pl.BlockSpec((None, d, tn), lambda i, j: (i, 0, j)),
            pl.BlockSpec((None, 1, tn), lambda i, j: (i, 0, j)),
        ],
        out_specs=pl.BlockSpec((None, bsz, tn), lambda i, j: (i, 0, j)),
        out_shape=jax.ShapeDtypeStruct((depth, bsz, n), F32),
        compiler_params=_params("parallel", "parallel"),
        name="ada_mod",
    )(c, ada_w, ada_b.reshape(depth, 1, n))
    return out.reshape(depth * bsz * N_MOD, 1, d)


def _ffn_kernel(*refs, nf, final_norm, cast_next):
    x_ref, shift_ref, scale_ref, gate_ref, g_ref, wa_ref, wb_ref, w2_ref = refs[:8]
    rest = list(refs[8:])
    fg_ref = rest.pop(0) if final_norm else None
    next_in = (rest.pop(0), rest.pop(0)) if cast_next else ()
    o_ref = rest.pop(0)
    next_out = (rest.pop(0), rest.pop(0)) if cast_next else ()
    (h_ref,) = rest

    f = pl.program_id(1)
    tm, d = x_ref.shape
    cc = min(COL_CHUNK, d)

    for src, dst in zip(next_in, next_out):
        dst[...] = src[...].astype(dst.dtype)

    half_gate = 0.5 * gate_ref[...]
    fuse_last = not final_norm

    def swiglu_rows(rows, first, last):
        h = h_ref[rows, :]
        a = _dot(h, wa_ref[...])
        b = _dot(h, wb_ref[...])
        act = (jax.nn.silu(a) * b).astype(BF16)
        for n in range(d // cc):
            cols = slice(n * cc, (n + 1) * cc)
            acc = _dot(act, w2_ref[:, cols])
            if not first:
                acc = o_ref[rows, cols] + acc
            if last:
                acc = x_ref[rows, cols] + half_gate[:, cols] * acc
            o_ref[rows, cols] = acc

    @pl.when(f == 0)
    def _():
        fc = min(FIRST_STEP_ROWS, tm)
        for c in range(tm // fc):
            _norm_mod_rows(x_ref, g_ref, shift_ref, scale_ref, h_ref, c * fc, fc)
            swiglu_rows(slice(c * fc, (c + 1) * fc), True, fuse_last and nf == 1)

    if nf > 1 and fuse_last:
        @pl.when(jnp.logical_and(f > 0, f < nf - 1))
        def _():
            swiglu_rows(slice(None), False, False)

        @pl.when(f == nf - 1)
        def _():
            swiglu_rows(slice(None), False, True)
    elif nf > 1:
        @pl.when(f > 0)
        def _():
            swiglu_rows(slice(None), False, False)

    if final_norm:
        @pl.when(f == nf - 1)
        def _():
            rc = NORM_ROWS

            def body(r, carry):
                rows = pl.ds(pl.multiple_of(r * rc, rc), rc)
                y = x_ref[rows, :] + half_gate * o_ref[rows, :]
                y = y * lax.rsqrt(jnp.mean(y * y, axis=-1, keepdims=True) + EPS) * fg_ref[...]
                o_ref[rows, :] = y
                return carry

            lax.fori_loop(0, tm // rc, body, 0)


def _ffn(x, mod, mod_base, norm_g, w13, w2, seq, next_w=None, final_g=None, tm=1024, tf=512):
    t, d = x.shape
    ff = w2.shape[0]
    tm = min(tm, seq)
    tf = min(tf, ff)
    tps = seq // tm
    nm = t // tm
    nf = ff // tf
    in_specs = [
        pl.BlockSpec((tm, d), lambda m, f: (m, 0)),
        _mod_spec(d, mod_base, 0, tps),
        _mod_spec(d, mod_base, 1, tps),
        _mod_spec(d, mod_base, 2, tps),
        pl.BlockSpec((1, d), lambda m, f: (0, 0)),
        pl.BlockSpec((d, tf), lambda m, f: (0, f)),
        pl.BlockSpec((d, tf), lambda m, f: (0, nf + f)),
        pl.BlockSpec((tf, d), lambda m, f: (f, 0)),
    ]
    args = [x, mod, mod, mod, norm_g, w13, w13, w2]
    out_specs = [pl.BlockSpec((tm, d), lambda m, f: (m, 0))]
    out_shape = [jax.ShapeDtypeStruct((t, d), F32)]
    if final_g is not None:
        in_specs.append(pl.BlockSpec((1, d), lambda m, f: (0, 0)))
        args.append(final_g)
    if next_w is not None:
        stack13, stack2, layer, which = next_w
        r13, c13 = d // nm, 2 * ff // nf
        r2, c2 = ff // nf, d // nm
        in_specs += [
            pl.BlockSpec((None, None, r13, c13), lambda m, f: (layer, which, m, f)),
            pl.BlockSpec((None, None, r2, c2), lambda m, f: (layer, which, f, m)),
        ]
        args += [stack13, stack2]
        out_specs += [pl.BlockSpec((r13, c13), lambda m, f: (m, f)),
                      pl.BlockSpec((r2, c2), lambda m, f: (f, m))]
        out_shape += [jax.ShapeDtypeStruct((d, 2 * ff), BF16),
                      jax.ShapeDtypeStruct((ff, d), BF16)]
    out = pl.pallas_call(
        functools.partial(_ffn_kernel, nf=nf, final_norm=final_g is not None,
                          cast_next=next_w is not None),
        grid=(nm, nf),
        in_specs=in_specs,
        out_specs=out_specs,
        out_shape=out_shape,
        scratch_shapes=[pltpu.VMEM((tm, d), BF16)],
        compiler_params=_params("parallel", "arbitrary"),
        name="ffn",
    )(*args)
    return out if next_w is not None else out[0]


def _proj_steps(x_ref, shift_ref, scale_ref, g_ref, h_ref, emit, emit_first=None):
    tm = x_ref.shape[0]

    @pl.when(pl.program_id(1) == 0)
    def _():
        fc = min(FIRST_STEP_ROWS, tm)
        for c in range(tm // fc):
            rows = slice(c * fc, (c + 1) * fc)
            _norm_mod_rows(x_ref, g_ref, shift_ref, scale_ref, h_ref, c * fc, fc)
            emit(rows)
            if emit_first is not None:
                emit_first(rows)

    @pl.when(pl.program_id(1) > 0)
    def _():
        emit(slice(None))


def _proj_kernel(x_ref, shift_ref, scale_ref, g_ref, w_ref, o_ref, h_ref):
    tn = o_ref.shape[1]
    cc = min(2 * COL_CHUNK, tn)

    def emit(rows):
        h = h_ref[rows, :]
        for n in range(tn // cc):
            cols = slice(n * cc, (n + 1) * cc)
            o_ref[rows, cols] = _dot(h, w_ref[:, cols]).astype(o_ref.dtype)

    _proj_steps(x_ref, shift_ref, scale_ref, g_ref, h_ref, emit)


def _proj_glu_kernel(x_ref, shift_ref, scale_ref, g_ref, wa_ref, wg_ref, o_ref, h_ref):
    def emit(rows):
        h = h_ref[rows, :]
        a = _dot(h, wa_ref[...])
        g = _dot(h, wg_ref[...])
        o_ref[rows, :] = (a * jax.nn.sigmoid(g)).astype(o_ref.dtype)

    _proj_steps(x_ref, shift_ref, scale_ref, g_ref, h_ref, emit)


def _proj_gates_kernel(x_ref, shift_ref, scale_ref, g_ref, w_ref, wgt_ref,
                       o_ref, gates_t_ref, h_ref):
    def emit(rows):
        o_ref[rows, :] = _dot(h_ref[rows, :], w_ref[...]).astype(o_ref.dtype)

    def emit_gates(rows):
        gates_t_ref[:, rows] = _dot_nt(wgt_ref[...], h_ref[rows, :])

    _proj_steps(x_ref, shift_ref, scale_ref, g_ref, h_ref, emit, emit_gates)


def _proj(x, mod, mod_base, norm_g, w, seq, out_dtype, tm=1024, tn=1024, glu=False,
          gate_w=None):
    t, d = x.shape
    n = w.shape[1] // 2 if glu else w.shape[1]
    tm = min(tm, seq)
    tn = min(tn // 2 if glu else tn, n)
    tps = seq // tm
    nn = n // tn
    x_spec = pl.BlockSpec((tm, d), lambda m, j: (m, 0))
    common = [x_spec, _mod_spec(d, mod_base, 0, tps), _mod_spec(d, mod_base, 1, tps),
              pl.BlockSpec((1, d), lambda m, j: (0, 0))]
    out_spec = pl.BlockSpec((tm, tn), lambda m, j: (m, j))
    out_shape = jax.ShapeDtypeStruct((t, n), out_dtype)
    scratch = [pltpu.VMEM((tm, d), BF16)]
    cp = _params("parallel", "arbitrary")
    if glu:
        return pl.pallas_call(
            _proj_glu_kernel, grid=(t // tm, nn),
            in_specs=common + [pl.BlockSpec((d, tn), lambda m, j: (0, j)),
                               pl.BlockSpec((d, tn), lambda m, j: (0, nn + j))],
            out_specs=out_spec, out_shape=out_shape, scratch_shapes=scratch,
            compiler_params=cp, name="proj_glu",
        )(x, mod, mod, norm_g, w, w)
    if gate_w is not None:
        ng = gate_w.shape[1]
        return pl.pallas_call(
            _proj_gates_kernel, grid=(t // tm, nn),
            in_specs=common + [pl.BlockSpec((d, tn), lambda m, j: (0, j)),
                               pl.BlockSpec((ng, d), lambda m, j: (0, 0))],
            out_specs=[out_spec, pl.BlockSpec((ng, tm), lambda m, j: (0, m))],
            out_shape=[out_shape, jax.ShapeDtypeStruct((ng, t), F32)],
            scratch_shapes=scratch, compiler_params=cp, name="proj_gates",
        )(x, mod, mod, norm_g, w, gate_w.T)
    return pl.pallas_call(
        _proj_kernel, grid=(t // tm, nn),
        in_specs=common + [pl.BlockSpec((d, tn), lambda m, j: (0, j))],
        out_specs=out_spec, out_shape=out_shape, scratch_shapes=scratch,
        compiler_params=cp, name="proj",
    )(x, mod, mod, norm_g, w)


def _gated_residual_cols(x_ref, u, gate_ref, w_ref, o_ref):
    d = x_ref.shape[1]
    cc = min(COL_CHUNK, d)
    for n in range(d // cc):
        cols = slice(n * cc, (n + 1) * cc)
        o_ref[:, cols] = x_ref[:, cols] + gate_ref[:, cols] * _dot(u, w_ref[:, cols])


def _outproj_kernel(x_ref, u_ref, gate_ref, w_ref, o_ref):
    _gated_residual_cols(x_ref, u_ref[...], gate_ref, w_ref, o_ref)


def _outproj(x, u, mod, mod_base, w, seq, tm=1024):
    t, d = x.shape
    k = u.shape[1]
    tm = min(tm, seq)
    tps = seq // tm
    return pl.pallas_call(
        _outproj_kernel,
        grid=(t // tm,),
        in_specs=[
            pl.BlockSpec((tm, d), lambda m: (m, 0)),
            pl.BlockSpec((tm, k), lambda m: (m, 0)),
            _mod_spec(d, mod_base, 2, tps),
            pl.BlockSpec((k, d), lambda m: (0, 0), pipeline_mode=pl.Buffered(1)),
        ],
        out_specs=pl.BlockSpec((tm, d), lambda m: (m, 0)),
        out_shape=jax.ShapeDtypeStruct((t, d), F32),
        compiler_params=_params("parallel"),
        name="outproj",
    )(x, u, mod, w)


def _conv_core_kernel(*refs, tiles_per_seq, n_casts):
    u_ref, halo_ref, dw_ref, dwb_ref, lng_ref, lnb_ref = refs[:6]
    cast_in = refs[6:6 + n_casts]
    o_ref = refs[6 + n_casts]
    cast_out = refs[7 + n_casts:7 + 2 * n_casts]
    ext_ref, y_ref = refs[7 + 2 * n_casts:]

    for src, dst in zip(cast_in, cast_out):
        dst[...] = src[...].astype(dst.dtype)

    tm, d = u_ref.shape
    first = (pl.program_id(0) % tiles_per_seq) == 0
    ext_ref[0:CONV_HALO, :] = jnp.where(first, 0.0, halo_ref[...])
    ext_ref[CONV_HALO:, :] = u_ref[...]

    rc = 128
    lead = CONV_HALO - (CONV_WIDTH - 1)
    wrows = rc + CONV_HALO
    for cb in range(d // 128):
        cols = slice(cb * 128, (cb + 1) * 128)
        bias = dwb_ref[:, cols]

        def body(r, carry, cols=cols, bias=bias):
            base = pl.multiple_of(r * rc, rc)
            win = ext_ref[pl.ds(base, wrows), cols]
            acc = jnp.broadcast_to(bias, (rc, 128))
            for res in range(8):
                shifted = pltpu.roll(win, wrows - res, axis=0) if res else win
                for k in range(CONV_WIDTH):
                    if (lead + k) % 8 == res:
                        start = lead + k - res
                        acc = acc + shifted[start:start + rc, :] * dw_ref[k:k + 1, cols]
            y_ref[pl.ds(base, rc), cols] = acc
            return carry

        lax.fori_loop(0, tm // rc, body, 0)

    lrc = min(ROW_CHUNK, tm)
    ln_g = lng_ref[...]
    ln_b = lnb_ref[...]

    def ln_body(r, carry):
        rows = pl.ds(pl.multiple_of(r * lrc, lrc), lrc)
        y = y_ref[rows, :]
        mu = jnp.mean(y, axis=-1, keepdims=True)
        yc = y - mu
        var = jnp.mean(yc * yc, axis=-1, keepdims=True)
        z = yc * lax.rsqrt(var + LN_EPS) * ln_g + ln_b
        o_ref[rows, :] = jax.nn.silu(z).astype(o_ref.dtype)
        return carry

    lax.fori_loop(0, tm // lrc, ln_body, 0)


def _conv_core(u, dw, dw_b, ln_g, ln_b, seq, casts=(), tm=CONV_TM):
    t, d = u.shape
    tm = min(tm, seq)
    tps = seq // tm
    hb = tm // CONV_HALO
    nm = t // tm
    in_cast_specs = [pl.BlockSpec((None, w.shape[1] // nm, nc), lambda m, j=j: (j, m, 0))
                     for w, j, nc in casts]
    out_cast_specs = [pl.BlockSpec((w.shape[1] // nm, nc), lambda m: (m, 0)) for w, _, nc in casts]
    out = pl.pallas_call(
        functools.partial(_conv_core_kernel, tiles_per_seq=tps, n_casts=len(casts)),
        grid=(nm,),
        in_specs=[
            pl.BlockSpec((tm, d), lambda m: (m, 0)),
            pl.BlockSpec((CONV_HALO, d), lambda m: (jnp.maximum(m * hb - 1, 0), 0)),
            pl.BlockSpec((CONV_WIDTH, d), lambda m: (0, 0)),
            pl.BlockSpec((1, d), lambda m: (0, 0)),
            pl.BlockSpec((1, d), lambda m: (0, 0)),
            pl.BlockSpec((1, d), lambda m: (0, 0)),
        ] + in_cast_specs,
        out_specs=[pl.BlockSpec((tm, d), lambda m: (m, 0))] + out_cast_specs,
        out_shape=[jax.ShapeDtypeStruct((t, d), BF16)]
        + [jax.ShapeDtypeStruct((w.shape[1], nc), BF16) for w, _, nc in casts],
        scratch_shapes=[pltpu.VMEM((tm + CONV_HALO, d), F32), pltpu.VMEM((tm, d), F32)],
        compiler_params=_params("parallel"),
        name="conv_core",
    )(u, u, dw, dw_b, ln_g, ln_b, *[w for w, _, _ in casts])
    return out[0], out[1:]


def _mlstm_kernel(qk_ref, v_ref, o_ref, gates_t_ref, bgt_ref, cw_ref, cb_ref, ng_ref, out_ref,
                  c_state, n_state, m_state, win_ref, *, dk, dv):
    lc = M_CHUNK
    nh = M_HEADS
    hist = 8

    @pl.when(pl.program_id(1) == 0)
    def _():
        c_state[...] = jnp.zeros_like(c_state)
        n_state[...] = jnp.zeros_like(n_state)
        m_state[...] = jnp.zeros_like(m_state)
        win_ref[0:hist, :] = jnp.zeros((hist, win_ref.shape[1]), F32)

    win_ref[hist:, :] = qk_ref[...]

    row = lax.broadcasted_iota(jnp.int32, (lc, lc), 0)
    col = lax.broadcasted_iota(jnp.int32, (lc, lc), 1)
    causal = col <= row
    tri_t = (row <= col).astype(BF16)

    g_row = gates_t_ref[...] + bgt_ref[...]
    li = g_row[:nh, :]
    b = _split_dot(jax.nn.log_sigmoid(g_row[nh:, :]), tri_t)
    m_prev = m_state[...]

    a = li - b
    lane = lax.broadcasted_iota(jnp.int32, (nh, lc), 1)
    run = a
    shift = 1
    while shift < lc:
        run = jnp.where(lane >= shift, jnp.maximum(run, pltpu.roll(run, shift, axis=1)), run)
        shift *= 2
    big = jnp.maximum(m_prev, run)
    w_inter = jnp.exp(m_prev - big)
    inv_floor = jnp.exp(-(b + big))

    b_last = jnp.broadcast_to(b[:, lc - 1:lc], (nh, lc))
    g = b_last - b + li
    m_new = jnp.maximum(b_last + m_prev, jnp.max(g, axis=-1, keepdims=True))
    decay = jnp.exp(b_last + m_prev - m_new)
    wg = jnp.exp(g - m_new)

    stacked = jnp.concatenate(
        [-big, w_inter, inv_floor, wg, jnp.zeros((lc - 4 * nh, lc), F32)], axis=0)
    cols = stacked.T

    def column(r, h):
        return cols[:, r * nh + h:r * nh + h + 1]

    qscale = dk ** -0.5

    def conv_silu(c):
        acc = jnp.broadcast_to(cb_ref[:, c], (lc, dk))
        for k in range(M_CONV_WIDTH):
            start = hist - (M_CONV_WIDTH - 1) + k
            acc = acc + win_ref[start:start + lc, c] * cw_ref[k:k + 1, c]
        return jax.nn.silu(acc)

    first = []
    for h in range(nh):
        q = conv_silu(slice(h * dk, (h + 1) * dk)) * qscale
        k = conv_silu(slice((nh + h) * dk, (nh + h + 1) * dk))
        q16 = q.astype(BF16)
        v16 = v_ref[:, h * dv:(h + 1) * dv].astype(BF16)
        kw = k * column(3, h)
        first.append(dict(
            v16=v16,
            qk=_dot_nt(q16, k.astype(BF16)),
            qc=_dot(q16, c_state[h].astype(BF16)),
            kv=_dot(kw.T.astype(BF16), v16),
            qn=jnp.sum(q * n_state[h:h + 1, :], axis=-1, keepdims=True),
            kw_sum=jnp.sum(kw, axis=0, keepdims=True)))

    second = []
    for h in range(nh):
        f = first[h]
        w_intra = jnp.exp(jnp.where(causal, column(0, h) + a[h:h + 1, :], -jnp.inf))
        s = f["qk"] * w_intra
        second.append(dict(s_sum=jnp.sum(s, axis=-1, keepdims=True),
                           sv=_dot(s.astype(BF16), f["v16"])))

    third = []
    for h in range(nh):
        f, s2 = first[h], second[h]
        num = column(1, h) * f["qc"] + s2["sv"]
        den = column(1, h) * f["qn"] + s2["s_sum"]
        floor = jnp.maximum(jnp.abs(den), column(2, h))
        scale = lax.rsqrt(jnp.mean(num * num, axis=-1, keepdims=True) + EPS * floor * floor)
        third.append((num, scale))

    for h in range(nh):
        num, scale = third[h]
        vcols = slice(h * dv, (h + 1) * dv)
        hn = num * scale * ng_ref[:, vcols]
        out_ref[:, vcols] = (hn * jax.nn.sigmoid(o_ref[:, vcols])).astype(out_ref.dtype)
        dec = decay[h:h + 1, 0:1]
        c_state[h] = dec * c_state[h] + first[h]["kv"]
        n_state[h:h + 1, :] = dec * n_state[h:h + 1, :] + first[h]["kw_sum"]

    m_state[...] = m_new
    win_ref[0:hist, :] = qk_ref[lc - hist:lc, :]


def _mlstm_core(proj, gates_t, b_gate, conv_w, conv_b, norm_g, bsz, seq, dk, dv):
    t = proj.shape[0]
    nh = M_HEADS
    nc = seq // M_CHUNK
    qkw = 2 * nh * dk
    vw = nh * dv
    assert qkw == vw, "projection column blocks assume equal q|k and v widths"
    ng = 2 * nh
    const = lambda r, c: pl.BlockSpec((r, c), lambda b, i: (0, 0))
    return pl.pallas_call(
        functools.partial(_mlstm_kernel, dk=dk, dv=dv),
        grid=(bsz, nc),
        in_specs=[
            pl.BlockSpec((M_CHUNK, qkw), lambda b, i: (b * nc + i, 0)),
            pl.BlockSpec((M_CHUNK, vw), lambda b, i: (b * nc + i, 1)),
            pl.BlockSpec((M_CHUNK, vw), lambda b, i: (b * nc + i, 2)),
            pl.BlockSpec((ng, M_CHUNK), lambda b, i: (0, b * nc + i)),
            const(ng, 1),
            const(M_CONV_WIDTH, qkw), const(1, qkw), const(1, vw),
        ],
        out_specs=pl.BlockSpec((M_CHUNK, vw), lambda b, i: (b * nc + i, 0)),
        out_shape=jax.ShapeDtypeStruct((t, vw), BF16),
        scratch_shapes=[
            pltpu.VMEM((nh, dk, dv), F32),
            pltpu.VMEM((nh, dk), F32),
            pltpu.VMEM((nh, 128), F32),
            pltpu.VMEM((M_CHUNK + 8, qkw), F32),
        ],
        compiler_params=_params("parallel", "arbitrary"),
        name="mlstm_core",
    )(proj, proj, proj, gates_t, b_gate.reshape(ng, 1), conv_w, conv_b, norm_g)


def _sb_kernel(q_ref, k_ref, v_ref, o_ref, *, scale, hd):
    tq = q_ref.shape[0]
    ng = q_ref.shape[1] // hd
    qi = pl.program_id(2)
    row = lax.broadcasted_iota(jnp.int32, (tq, tq), 0)
    col = lax.broadcasted_iota(jnp.int32, (tq, tq), 1)
    strict = col < row
    after = (row > col).astype(BF16)
    after2 = jnp.concatenate([after, after], axis=0)

    def tile(kj, state, diag):
        rows = pl.ds(pl.multiple_of(kj * tq, tq), tq)
        heads = [slice(g * hd, (g + 1) * hd) for g in range(ng)]
        zs = [_dot_nt(q_ref[:, cols], k_ref[rows, cols]) * (scale * LOG2E) for cols in heads]
        mids = []
        for g in range(ng):
            z = zs[g]
            log_beta = jnp.minimum(z, 0.0) - jnp.log2(1.0 + jnp.exp2(-jnp.abs(z)))
            log_keep = log_beta - z
            if diag:
                log_keep = jnp.where(strict, log_keep, 0.0)
            hi = log_keep.astype(BF16)
            lo = (log_keep - hi.astype(F32)).astype(BF16)
            within = _dot(jnp.concatenate([hi, lo], axis=1), after2)
            mids.append((log_beta, within, within[:, 0:1] + log_keep[:, 0:1]))
        out = []
        for g in range(ng):
            carry, acc = state[g]
            log_beta, within, row_sum = mids[g]
            a = jnp.exp2(log_beta + within + carry)
            if diag:
                a = jnp.where(strict, a, 0.0)
            acc = acc + _dot(a.astype(BF16), v_ref[rows, heads[g]])
            out.append((carry + row_sum, acc))
        return tuple(out)

    def any_live(state):
        top = state[0][0]
        for g in range(1, ng):
            top = jnp.maximum(top, state[g][0])
        return jnp.max(top) > SB_DEAD_LOG2

    init = tuple((jnp.zeros((tq, 1), F32), jnp.zeros((tq, hd), F32)) for _ in range(ng))
    state = tile(qi, init, True)

    def cond(loop):
        i, live, _ = loop
        return jnp.logical_and(i < qi, live)

    def body(loop):
        i, _, st = loop
        st = tile(qi - 1 - i, st, False)
        return i + 1, any_live(st), st

    _, _, state = lax.while_loop(cond, body, (jnp.int32(0), jnp.bool_(True), state))
    for g in range(ng):
        o_ref[:, g * hd:(g + 1) * hd] = state[g][1].astype(o_ref.dtype)


def _sb_core(qkv, bsz, seq, hd, tq=256, heads_per_step=8):
    t = qkv.shape[0]
    nh = SB_HEADS
    tq = min(tq, seq)
    nq = seq // tq
    ngrp = nh // heads_per_step
    gw = heads_per_step * hd
    return pl.pallas_call(
        functools.partial(_sb_kernel, scale=hd ** -0.5, hd=hd),
        grid=(bsz, ngrp, nq),
        in_specs=[
            pl.BlockSpec((tq, gw), lambda b, h, i: (b * nq + i, h)),
            pl.BlockSpec((seq, gw), lambda b, h, i: (b, ngrp + h)),
            pl.BlockSpec((seq, gw), lambda b, h, i: (b, 2 * ngrp + h)),
        ],
        out_specs=pl.BlockSpec((tq, gw), lambda b, h, i: (b * nq + i, h)),
        out_shape=jax.ShapeDtypeStruct((t, nh * hd), BF16),
        compiler_params=_params("parallel", "parallel", "arbitrary"),
        name="sb_core",
    )(qkv, qkv, qkv)


def _pool_kernel(x_ref, u_ref, halo_ref, gate_ref, wg_ref, sc_ref, wo_ref, o_ref,
                 ext_ref, y_ref, *, tiles_per_seq):
    tm, d = u_ref.shape
    ngrp = len(POOL_WINDOWS)
    gw = d // ngrp
    tile_in_seq = pl.program_id(0) % tiles_per_seq
    ext_ref[0:POOL_HALO, :] = jnp.where(tile_in_seq == 0, 0.0, halo_ref[...])
    ext_ref[POOL_HALO:, :] = u_ref[...]
    pos = (tile_in_seq * tm + 1 + lax.broadcasted_iota(jnp.int32, (tm, 1), 0)).astype(F32)

    for gi, w in enumerate(POOL_WINDOWS):
        inv = 1.0 / jnp.minimum(pos, float(w))
        for cb in range(gw // 128):
            lo = gi * gw + cb * 128
            cols = slice(lo, lo + 128)
            ssum = ext_ref[:, cols]
            shift = 1
            while shift < w:
                ssum = ssum + pltpu.roll(ssum, shift, axis=0)
                shift *= 2
            uf = u_ref[:, cols]
            p = ssum[POOL_HALO:, :] * inv - uf
            y_ref[:, cols] = p.astype(y_ref.dtype)

    for gi in range(ngrp):
        cols = slice(gi * gw, (gi + 1) * gw)
        yg = _dot(y_ref[:, cols], wg_ref[gi]) * sc_ref[:, cols]
        y_ref[:, cols] = yg.astype(y_ref.dtype)

    _gated_residual_cols(x_ref, y_ref[...], gate_ref, wo_ref, o_ref)


def _pool_mix(x, u, mod, mod_base, w_grp, ch_scale, w_out, seq, tm=512):
    t, d = x.shape
    tm = min(tm, seq)
    tps = seq // tm
    hb = tm // POOL_HALO
    ngrp, gw, _ = w_grp.shape
    single = pl.Buffered(1)
    return pl.pallas_call(
        functools.partial(_pool_kernel, tiles_per_seq=tps),
        grid=(t // tm,),
        in_specs=[
            pl.BlockSpec((tm, d), lambda m: (m, 0)),
            pl.BlockSpec((tm, d), lambda m: (m, 0)),
            pl.BlockSpec((POOL_HALO, d), lambda m: (jnp.maximum(m * hb - 1, 0), 0)),
            _mod_spec(d, mod_base, 2, tps),
            pl.BlockSpec((ngrp, gw, gw), lambda m: (0, 0, 0), pipeline_mode=single),
            pl.BlockSpec((1, d), lambda m: (0, 0)),
            pl.BlockSpec((d, d), lambda m: (0, 0), pipeline_mode=single),
        ],
        out_specs=pl.BlockSpec((tm, d), lambda m: (m, 0)),
        out_shape=jax.ShapeDtypeStruct((t, d), F32),
        scratch_shapes=[pltpu.VMEM((tm + POOL_HALO, d), F32), pltpu.VMEM((tm, d), BF16)],
        compiler_params=_params("parallel"),
        name="pool_mix",
    )(x, u, u, mod, w_grp, ch_scale, w_out)


def kernel(x, c, ada_w, ada_b, norm_g, ffn_w13, ffn_w2, conv_w_in, conv_dw, conv_dw_b, conv_ln_g, conv_ln_b, conv_w_out, m_w_in, m_b_gate, m_conv_w, m_conv_b, m_norm_g, m_w_out, sb_w_in, sb_w_out, p_w_in, p_w_grp, p_scale, p_w_out, final_g):
    bsz, seq, d = x.shape
    depth = ada_w.shape[0]
    n_mixers = 4
    t = bsz * seq
    row = lambda a: a.reshape(1, -1)

    mod = _ada_mod(c, ada_w, ada_b)
    xt = x.reshape(t, d)
    w13 = ffn_w13[0, 0].astype(BF16)
    w2 = ffn_w2[0, 0].astype(BF16)

    m_wide = m_w_in.shape[2] - 2 * M_HEADS

    def mixer_mats(i):
        mix, j = i % n_mixers, i // n_mixers
        if mix == 0:
            mats = {"w_in": conv_w_in, "w_out": conv_w_out}
        elif mix == 1:
            return {"w_in": (m_w_in, j, m_wide), "w_out": (m_w_out, j, d)}
        elif mix == 2:
            mats = {"w_in": sb_w_in, "w_out": sb_w_out}
        else:
            grp = p_w_grp.shape[-1]
            mats = {"w_in": p_w_in, "w_out": p_w_out,
                    "w_grp": p_w_grp.reshape(p_w_grp.shape[0], -1, grp)}
        return {name: (w, j, w.shape[2]) for name, w in mats.items()}

    half = {}

    for i in range(depth):
        mix, j = i % n_mixers, i // n_mixers
        base = i * bsz * N_MOD

        xt, w13, w2 = _ffn(xt, mod, base, row(norm_g[i, 0]), w13, w2, seq,
                           next_w=(ffn_w13, ffn_w2, i, 1))

        mbase = base + 3
        g1 = row(norm_g[i, 1])
        wts = {name: half[(i, name)] if (i, name) in half else w[jj, :, :nc].astype(BF16)
               for name, (w, jj, nc) in mixer_mats(i).items()}
        if mix == 0:
            u = _proj(xt, mod, mbase, g1, wts["w_in"], seq, F32, glu=True)
            steps = t // min(CONV_TM, seq)
            later = [(k, name, spec) for k in range(i + 1, depth)
                     for name, spec in mixer_mats(k).items()
                     if (k, name) not in half and spec[0].shape[1] % (16 * steps) == 0]
            u, converted = _conv_core(u, conv_dw[j], row(conv_dw_b[j]), row(conv_ln_g[j]),
                                      row(conv_ln_b[j]), seq,
                                      casts=[spec for _, _, spec in later])
            half.update({(k, name): wb for (k, name, _), wb in zip(later, converted)})
            xt = _outproj(xt, u, mod, mbase, wts["w_out"], seq)
        elif mix == 1:
            dv = d // M_HEADS
            dk = (m_wide - 2 * d) // (2 * M_HEADS)
            proj, gates_t = _proj(xt, mod, mbase, g1, wts["w_in"], seq, F32,
                                  gate_w=m_w_in[j][:, m_wide:].astype(BF16))
            u = _mlstm_core(proj, gates_t, m_b_gate[j], m_conv_w[j],
                            row(m_conv_b[j]), row(m_norm_g[j]), bsz, seq, dk, dv)
            xt = _outproj(xt, u, mod, mbase, wts["w_out"], seq)
        elif mix == 2:
            qkv = _proj(xt, mod, mbase, g1, wts["w_in"], seq, BF16, tn=2048)
            u = _sb_core(qkv, bsz, seq, d // SB_HEADS)
            xt = _outproj(xt, u, mod, mbase, wts["w_out"], seq)
        else:
            u = _proj(xt, mod, mbase, g1, wts["w_in"], seq, F32)
            xt = _pool_mix(xt, u, mod, mbase, wts["w_grp"].reshape(p_w_grp.shape[1:]),
                           row(p_scale[j]), wts["w_out"], seq)

        g2 = row(norm_g[i, 2])
        if i + 1 < depth:
            xt, w13, w2 = _ffn(xt, mod, base + 6, g2, w13, w2, seq,
                               next_w=(ffn_w13, ffn_w2, i + 1, 0))
        else:
            xt = _ffn(xt, mod, base + 6, g2, w13, w2, seq, final_g=row(final_g))

    return xt.reshape(bsz, seq, d)
```
